```python
import math
import jax
import jax.numpy as jnp
from jax import lax
import numpy as np

D_MODEL = 2048
BATCH = 4
SEQ = 2048
DEPTH = 4
DEC_BATCH = 8
DEC_SEQ = 4
PAST_LEN = 16384
PAGE_SIZE = 128

ATTN_WIDTH = D_MODEL // 2
CONV_WIDTH = D_MODEL - ATTN_WIDTH
N_HEADS = 8
HEAD_DIM = ATTN_WIDTH // N_HEADS // 2
V_DIM = 2 * HEAD_DIM
CONV_K = 3
D_FF = -(-(8 * D_MODEL) // (3 * 256)) * 256
IN_COLS = 3 * ATTN_WIDTH + 3 * CONV_WIDTH
ROPE_THETA = 10000.0
RMS_EPS = 1e-6
SUBLN_EPS = 1e-5
ATTN_SCALE = HEAD_DIM ** -0.5
Q_BLOCK = 128

kernel_name = "hybrid_diffattn_shortconv_decode_step"


def _rmsnorm(x, g, eps):
    xf = x.astype(jnp.float32)
    y = xf * lax.rsqrt(jnp.mean(xf * xf, axis=-1, keepdims=True) + eps)
    return (y * g.astype(jnp.float32)).astype(x.dtype)


def _rope(x, pos):
    inv = 1.0 / (ROPE_THETA ** (jnp.arange(0, HEAD_DIM, 2, dtype=jnp.float32) / HEAD_DIM))
    ang = pos.astype(jnp.float32)[:, None] * inv[None, :]
    ang = jnp.concatenate([ang, ang], axis=-1)
    cos = jnp.cos(ang)[None, :, None, None, :]
    sin = jnp.sin(ang)[None, :, None, None, :]
    xf = x.astype(jnp.float32)
    half = HEAD_DIM // 2
    rot = jnp.concatenate([-xf[..., half:], xf[..., :half]], axis=-1)
    return (xf * cos + rot * sin).astype(x.dtype)


def _project(h, w_in, pos):
    B, T, _ = h.shape
    z = jnp.einsum('btd,dc->btc', h, w_in)
    a, c = ATTN_WIDTH, CONV_WIDTH
    q = z[..., :a].reshape(B, T, N_HEADS, 2, HEAD_DIM)
    k = z[..., a:2 * a].reshape(B, T, N_HEADS, 2, HEAD_DIM)
    v = z[..., 2 * a:3 * a].reshape(B, T, N_HEADS, V_DIM)
    gb = z[..., 3 * a:3 * a + c]
    gc = z[..., 3 * a + c:3 * a + 2 * c]
    hc = z[..., 3 * a + 2 * c:]
    return _rope(q, pos), _rope(k, pos), v, gb, gc, hc


def _lambda(lq1, lk1, lq2, lk2, lam_init):
    f = jnp.float32
    return (jnp.exp(jnp.sum(lq1.astype(f) * lk1.astype(f)))
            - jnp.exp(jnp.sum(lq2.astype(f) * lk2.astype(f))) + lam_init)


def _diff_attend(q, k, v, mask, lam):
    s = jnp.einsum('bqhmd,bkhmd->bhmqk', q, k).astype(jnp.float32) * ATTN_SCALE
    s = jnp.where(mask, s, -jnp.inf)
    p = jax.nn.softmax(s, axis=-1)
    a = p[:, :, 0] - lam * p[:, :, 1]
    return jnp.einsum('bhqk,bkhe->bqhe', a.astype(v.dtype), v)


def _prompt_attention(q, k, v, lam):
    B, T = q.shape[0], q.shape[1]
    nb = T // Q_BLOCK
    qb = jnp.moveaxis(q.reshape(B, nb, Q_BLOCK, N_HEADS, 2, HEAD_DIM), 1, 0)
    kpos = jnp.arange(T, dtype=jnp.int32)

    def one_block(args):
        qi, i = args
        qpos = i * Q_BLOCK + jnp.arange(Q_BLOCK, dtype=jnp.int32)
        mask = kpos[None, :] <= qpos[:, None]
        return _diff_attend(qi, k, v, mask, lam)

    o = lax.map(one_block, (qb, jnp.arange(nb, dtype=jnp.int32)))
    return jnp.moveaxis(o, 0, 1).reshape(B, T, N_HEADS, V_DIM)


def _head_out(o, g, lam_init):
    B, T = o.shape[0], o.shape[1]
    o = _rmsnorm(o, g, SUBLN_EPS) * (1.0 - lam_init)
    return o.reshape(B, T, ATTN_WIDTH)


def _short_conv(gb, gc, hc, prev, w):
    T = hc.shape[1]
    u = gc * hc
    up = jnp.concatenate([prev.astype(u.dtype), u], axis=1)
    conv = w[0] * up[:, 0:T]
    for j in range(1, CONV_K):
        conv = conv + w[j] * up[:, j:j + T]
    return gb * conv, up[:, -(CONV_K - 1):]


def _merge_ffn(x, attn_o, yc, w_out, ffn_g, w_gate, w_up, w_down):
    mix = jnp.concatenate([attn_o, yc], axis=-1)
    x = x + jnp.einsum('btc,cd->btd', mix, w_out)
    h = _rmsnorm(x, ffn_g, RMS_EPS)
    ff = jax.nn.silu(jnp.einsum('btd,df->btf', h, w_gate)) * jnp.einsum('btd,df->btf', h, w_up)
    return x + jnp.einsum('btf,fd->btd', ff, w_down)


def setup_inputs(seed: int = 0) -> dict:
    key = jax.random.key(seed)
    ks = jax.random.split(key, 24)
    f = jnp.float32
    n_pages = PAST_LEN // PAGE_SIZE
    n_used = DEC_BATCH * n_pages
    n_pool = n_used + max(1, n_used // 4)

    def nrm(k, shape, scale):
        return jax.random.normal(k, shape, f) * scale

    x_prompt = nrm(ks[0], (BATCH, SEQ, D_MODEL), 1.0)
    x_sample = nrm(ks[1], (DEC_BATCH, DEC_SEQ, D_MODEL), 1.0)
    cache_k = nrm(ks[2], (DEPTH, n_pool, PAGE_SIZE, N_HEADS, 2 * HEAD_DIM), 1.0)
    cache_v = nrm(ks[3], (DEPTH, n_pool, PAGE_SIZE, N_HEADS, V_DIM), 1.0)
    state_conv = nrm(ks[4], (DEPTH, DEC_BATCH, CONV_K - 1, CONV_WIDTH), 1.0)
    page_table = jax.random.permutation(ks[5], n_pool)[:n_used].reshape(DEC_BATCH, n_pages).astype(jnp.int32)
    attn_norm = 1.0 + nrm(ks[6], (DEPTH, D_MODEL), 0.02)
    w_in = nrm(ks[7], (DEPTH, D_MODEL, IN_COLS), D_MODEL ** -0.5)
    conv_w = nrm(ks[8], (DEPTH, CONV_K, CONV_WIDTH), CONV_K ** -0.5)
    lambda_q1 = nrm(ks[9], (DEPTH, HEAD_DIM), 0.1)
    lambda_k1 = nrm(ks[10], (DEPTH, HEAD_DIM), 0.1)
    lambda_q2 = nrm(ks[11], (DEPTH, HEAD_DIM), 0.1)
    lambda_k2 = nrm(ks[12], (DEPTH, HEAD_DIM), 0.1)
    subln_g = 1.0 + nrm(ks[13], (DEPTH, V_DIM), 0.02)
    w_out = nrm(ks[14], (DEPTH, D_MODEL, D_MODEL), D_MODEL ** -0.5)
    ffn_norm = 1.0 + nrm(ks[15], (DEPTH, D_MODEL), 0.02)
    w_gate = nrm(ks[16], (DEPTH, D_MODEL, D_FF), D_MODEL ** -0.5)
    w_up = nrm(ks[17], (DEPTH, D_MODEL, D_FF), D_MODEL ** -0.5)
    w_down = nrm(ks[18], (DEPTH, D_FF, D_MODEL), D_FF ** -0.5)
    final_norm = 1.0 + nrm(ks[19], (D_MODEL,), 0.02)
    return {"x_prompt": x_prompt, "x_sample": x_sample, "cache_k": cache_k, "cache_v": cache_v,
            "state_conv": state_conv, "page_table": page_table, "attn_norm": attn_norm, "w_in": w_in,
            "conv_w": conv_w, "lambda_q1": lambda_q1, "lambda_k1": lambda_k1, "lambda_q2": lambda_q2,
            "lambda_k2": lambda_k2, "subln_g": subln_g, "w_out": w_out, "ffn_norm": ffn_norm,
            "w_gate": w_gate, "w_up": w_up, "w_down": w_down, "final_norm": final_norm}


def reference(x_prompt, x_sample, cache_k, cache_v, state_conv, page_table, attn_norm, w_in, conv_w,
              lambda_q1, lambda_k1, lambda_q2, lambda_k2, subln_g, w_out, ffn_norm, w_gate, w_up,
              w_down, final_norm):
    B, T, _ = x_prompt.shape
    DB, S, _ = x_sample.shape
    n_pages = page_table.shape[1]
    past = n_pages * PAGE_SIZE
    pos_p = jnp.arange(T, dtype=jnp.int32)
    pos_s = past + jnp.arange(S, dtype=jnp.int32)
    mask_s = jnp.arange(past + S, dtype=jnp.int32)[None, :] <= pos_s[:, None]

    xp, xs = x_prompt, x_sample
    kp_l, vp_l, cp_l, ks_l, vs_l, cs_l = [], [], [], [], [], []
    for l in range(DEPTH):
        lam_init = 0.8 - 0.6 * math.exp(-0.3 * l)
        lam = _lambda(lambda_q1[l], lambda_k1[l], lambda_q2[l], lambda_k2[l], lam_init)

        h = _rmsnorm(xp, attn_norm[l], RMS_EPS)
        q, k, v, gb, gc, hc = _project(h, w_in[l], pos_p)
        ao = _head_out(_prompt_attention(q, k, v, lam), subln_g[l], lam_init)
        yc, cst = _short_conv(gb, gc, hc, jnp.zeros((B, CONV_K - 1, CONV_WIDTH), hc.dtype), conv_w[l])
        xp = _merge_ffn(xp, ao, yc, w_out[l], ffn_norm[l], w_gate[l], w_up[l], w_down[l])
        kp_l.append(k.reshape(B, T, N_HEADS, 2 * HEAD_DIM))
        vp_l.append(v)
        cp_l.append(cst)

        h = _rmsnorm(xs, attn_norm[l], RMS_EPS)
        q, k, v, gb, gc, hc = _project(h, w_in[l], pos_s)
        k_past = cache_k[l][page_table].reshape(DB, past, N_HEADS, 2, HEAD_DIM)
        v_past = cache_v[l][page_table].reshape(DB, past, N_HEADS, V_DIM)
        k_all = jnp.concatenate([k_past.astype(k.dtype), k], axis=1)
        v_all = jnp.concatenate([v_past.astype(v.dtype), v], axis=1)
        ao = _head_out(_diff_attend(q, k_all, v_all, mask_s, lam), subln_g[l], lam_init)
        yc, cst = _short_conv(gb, gc, hc, state_conv[l], conv_w[l])
        xs = _merge_ffn(xs, ao, yc, w_out[l], ffn_norm[l], w_gate[l], w_up[l], w_down[l])
        ks_l.append(k.reshape(DB, S, N_HEADS, 2 * HEAD_DIM))
        vs_l.append(v)
        cs_l.append(cst)

    y_prompt = _rmsnorm(xp, final_norm, RMS_EPS)
    y_sample = _rmsnorm(xs, final_norm, RMS_EPS)
    return (y_prompt, y_sample, jnp.stack(kp_l), jnp.stack(vp_l), jnp.stack(cp_l),
            jnp.stack(ks_l), jnp.stack(vs_l), jnp.stack(cs_l))
```

```python
import functools
import math

import numpy as np
import jax
import jax.numpy as jnp
from jax import lax
from jax.experimental import pallas as pl
from jax.experimental.pallas import tpu as pltpu

ROPE_THETA = 10000.0
RMS_EPS = 1e-6
SUBLN_EPS = 1e-5
CONV_K = 3
MASK_VALUE = -1e30

V7X_LANES = 128
V7X_SUBLANES = 8
V7X_VMEM_BYTES = 64 * 1024 * 1024
VMEM_TEMP_BYTES = 12 * 1024 * 1024

F32 = jnp.float32
BF16 = jnp.bfloat16


def _compiler_params(semantics, block_bytes):
    limit = min(block_bytes + VMEM_TEMP_BYTES, V7X_VMEM_BYTES - 6 * 1024 * 1024)
    return pltpu.CompilerParams(dimension_semantics=semantics, vmem_limit_bytes=int(limit))


def _nbytes(shape, dtype):
    return int(np.prod(shape)) * jnp.dtype(dtype).itemsize


def _rms_scale(x, eps):
    return x * lax.rsqrt(jnp.mean(x * x, axis=-1, keepdims=True) + eps)


def _lambda_value(lamv_ref, lam_init):
    d1 = jnp.sum(lamv_ref[0:1, :] * lamv_ref[1:2, :], axis=1, keepdims=True)
    d2 = jnp.sum(lamv_ref[2:3, :] * lamv_ref[3:4, :], axis=1, keepdims=True)
    return jnp.exp(d1) - jnp.exp(d2) + lam_init


def _rope_tables(positions, head_dim):
    inv = 1.0 / (ROPE_THETA ** (np.arange(0, head_dim, 2, dtype=np.float64) / head_dim))
    ang = np.asarray(positions, np.float64)[:, None] * inv[None, :]
    ang = np.concatenate([ang, ang], axis=-1)
    sign = np.concatenate([-np.ones(head_dim // 2), np.ones(head_dim // 2)])
    reps = V7X_LANES // head_dim
    cos = np.tile(np.cos(ang), (1, reps)).astype(np.float32)
    sin = np.tile(np.sin(ang) * sign[None, :], (1, reps)).astype(np.float32)
    return jnp.asarray(cos), jnp.asarray(sin)


def _rope(zc, cos, sin, low, half):
    ahead = pltpu.roll(zc, V7X_LANES - half, axis=1)
    behind = pltpu.roll(zc, half, axis=1)
    return zc * cos + jnp.where(low, ahead, behind) * sin


def _in_proj_kernel(x_ref, g_ref, w_ref, cos_ref, sin_ref,
                    q_ref, kb_ref, vb_ref, k_ref, v_ref, gb_ref, gc_ref, hc_ref, h_scr,
                    *, half, q_scale):
    n = pl.program_id(1)
    bm, W = q_ref.shape
    E = k_ref.shape[1]
    H = W // E

    @pl.when(n == 0)
    def _():
        h_scr[...] = (_rms_scale(x_ref[...], RMS_EPS) * g_ref[...]).astype(BF16)

    z = jnp.dot(h_scr[...], w_ref[...], preferred_element_type=F32)
    lane = lax.broadcasted_iota(jnp.int32, (bm, E), 1)
    low = (lane % (2 * half)) < half

    @pl.when(n == 0)
    def _():
        for h in range(H):
            r = _rope(z[:, h * E:(h + 1) * E], cos_ref[...], sin_ref[...], low, half)
            q_ref[:, h * E:(h + 1) * E] = (r * q_scale).astype(BF16)

    @pl.when(n == 1)
    def _():
        for h in range(H):
            r = _rope(z[:, h * E:(h + 1) * E], cos_ref[...], sin_ref[...], low, half)
            kb_ref[:, h * E:(h + 1) * E] = r.astype(BF16)
            k_ref[pl.ds(h, bm, stride=H), :] = r

    @pl.when(n == 2)
    def _():
        vb_ref[...] = z.astype(BF16)
        for h in range(H):
            v_ref[pl.ds(h, bm, stride=H), :] = z[:, h * E:(h + 1) * E]

    for idx, ref in ((3, gb_ref), (4, gc_ref), (5, hc_ref)):
        @pl.when(n == idx)
        def _(ref=ref):
            ref[...] = z


def _in_proj(x, g, w, cos, sin, *, bm, n_heads):
    M, D = x.shape
    W = w.shape[1] // 6
    H = n_heads
    E = W // H
    head_dim = E // 2
    n_pos_blocks = cos.shape[0] // bm
    rows = pl.BlockSpec((bm, W), lambda m, n: (m, 0))
    head_rows = pl.BlockSpec((bm * H, E), lambda m, n: (m, 0))
    out_shape = [jax.ShapeDtypeStruct((M, W), BF16)] * 3 + [jax.ShapeDtypeStruct((M * H, E), F32)] * 2 \
        + [jax.ShapeDtypeStruct((M, W), F32)] * 3
    block_bytes = 2 * (_nbytes((bm, D), F32) + _nbytes((D, W), BF16) + 3 * _nbytes((bm, W), BF16)
                       + 5 * _nbytes((bm, W), F32)) + _nbytes((bm, D), BF16)
    return pl.pallas_call(
        functools.partial(_in_proj_kernel, half=head_dim // 2, q_scale=head_dim ** -0.5),
        out_shape=out_shape,
        grid=(M // bm, 6),
        in_specs=[
            pl.BlockSpec((bm, D), lambda m, n: (m, 0)),
            pl.BlockSpec((1, D), lambda m, n: (0, 0)),
            pl.BlockSpec((D, W), lambda m, n: (0, n)),
            pl.BlockSpec((bm, V7X_LANES), lambda m, n: (m % n_pos_blocks, 0)),
            pl.BlockSpec((bm, V7X_LANES), lambda m, n: (m % n_pos_blocks, 0)),
        ],
        out_specs=[rows] * 3 + [head_rows] * 2 + [rows] * 3,
        scratch_shapes=[pltpu.VMEM((bm, D), BF16)],
        compiler_params=_compiler_params(("arbitrary", "arbitrary"), block_bytes),
        name="in_proj",
    )(x, g.reshape(1, D), w, cos, sin)


def _head_finish(acc, l, lam, g, out_scale, rows):
    o = acc[0:rows] / l[0:rows] - lam * (acc[rows:2 * rows] / l[rows:2 * rows])
    return (_rms_scale(o, SUBLN_EPS) * g) * out_scale


def _prompt_attn_kernel(lamv_ref, g_ref, q_ref, k_ref, v_ref, o_ref,
                        qs_scr, m_scr, l_scr, acc_scr, *, blk, lam_init):
    T, E = q_ref.shape
    half = E // 2
    lam = _lambda_value(lamv_ref, lam_init)
    lane = lax.broadcasted_iota(jnp.int32, (blk, E), 1)
    row = lax.broadcasted_iota(jnp.int32, (2 * blk, blk), 0)
    col = lax.broadcasted_iota(jnp.int32, (2 * blk, blk), 1)
    causal = col <= jnp.where(row >= blk, row - blk, row)

    def step(ki, masked):
        start = ki * blk if isinstance(ki, int) else pl.multiple_of(ki * blk, blk)
        k = k_ref[pl.ds(start, blk), :]
        v = v_ref[pl.ds(start, blk), :]
        s = lax.dot_general(qs_scr[...], k, (((1,), (1,)), ((), ())), preferred_element_type=F32)
        if masked:
            s = jnp.where(causal, s, MASK_VALUE)
        m_old = m_scr[...]
        m_new = jnp.maximum(m_old, jnp.max(s, axis=1, keepdims=True))
        alpha = jnp.exp(m_old - m_new)
        p = jnp.exp(s - m_new)
        l_scr[...] = alpha * l_scr[...] + jnp.sum(p, axis=1, keepdims=True)
        acc_scr[...] = alpha * acc_scr[...] + jnp.dot(p.astype(BF16), v, preferred_element_type=F32)
        m_scr[...] = m_new

    for qi in range(T // blk):
        q = q_ref[qi * blk:(qi + 1) * blk, :]
        qs_scr[0:blk, :] = jnp.where(lane < half, q, jnp.zeros_like(q))
        qs_scr[blk:2 * blk, :] = jnp.where(lane >= half, q, jnp.zeros_like(q))
        m_scr[...] = jnp.full(m_scr.shape, MASK_VALUE, F32)
        l_scr[...] = jnp.zeros(l_scr.shape, F32)
        acc_scr[...] = jnp.zeros(acc_scr.shape, F32)
        if qi > 0:
            def body(ki, carry):
                step(ki, False)
                return carry
            lax.fori_loop(0, qi, body, 0)
        step(qi, True)
        o = _head_finish(acc_scr[...], l_scr[...], lam, g_ref[...], 1.0 - lam_init, blk)
        o_ref[qi * blk:(qi + 1) * blk, :] = o.astype(o_ref.dtype)


def _prompt_attn(q, k, v, lamv, g, *, batch, blk, lam_init):
    M, W = q.shape
    E = g.shape[0]
    H = W // E
    T = M // batch
    head = lambda b, h: (b, h)
    block_bytes = 2 * 4 * _nbytes((T, E), BF16) + _nbytes((2 * blk, E), BF16) \
        + 2 * _nbytes((2 * blk, V7X_LANES), F32) + _nbytes((2 * blk, E), F32)
    return pl.pallas_call(
        functools.partial(_prompt_attn_kernel, blk=blk, lam_init=lam_init),
        out_shape=jax.ShapeDtypeStruct((M, W), BF16),
        grid=(batch, H),
        in_specs=[
            pl.BlockSpec(lamv.shape, lambda b, h: (0, 0)),
            pl.BlockSpec((1, E), lambda b, h: (0, 0)),
            pl.BlockSpec((T, E), head),
            pl.BlockSpec((T, E), head),
            pl.BlockSpec((T, E), head),
        ],
        out_specs=pl.BlockSpec((T, E), head),
        scratch_shapes=[
            pltpu.VMEM((2 * blk, E), BF16),
            pltpu.VMEM((2 * blk, 1), F32), pltpu.VMEM((2 * blk, 1), F32), pltpu.VMEM((2 * blk, E), F32),
        ],
        compiler_params=_compiler_params(("arbitrary", "arbitrary"), block_bytes),
        name="prompt_attn",
    )(lamv, g.reshape(1, E), q, k, v)


def _sample_attn_kernel(pt_ref, lamv_ref, g_ref, q_ref, kn_ref, vn_ref, *refs,
                        pages_per_step, lam_init):
    P = pages_per_step
    k_refs, v_refs = refs[:P], refs[P:2 * P]
    o_ref, qall_scr, bias_scr, m_scr, l_scr, acc_scr = refs[2 * P:]
    S, H, E = q_ref.shape
    half = E // 2
    i = pl.program_id(1)

    @pl.when(i == 0)
    def _():
        rr = lax.broadcasted_iota(jnp.int32, bias_scr.shape, 0)
        cc = lax.broadcasted_iota(jnp.int32, bias_scr.shape, 1)
        bias_scr[...] = jnp.where(cc % H == rr % H, 0.0, MASK_VALUE)
        lane = lax.broadcasted_iota(jnp.int32, (H, E), 1)
        for m in range(2):
            for s in range(S):
                rows = pl.ds((m * S + s) * H, H)
                q = jnp.where(lane // half == m, q_ref[s], 0.0)
                qall_scr[rows, :] = q.astype(BF16)
                sc = [jnp.sum(q * kn_ref[j], axis=1, keepdims=True) for j in range(s + 1)]
                m0 = functools.reduce(jnp.maximum, sc)
                ps = [jnp.exp(c - m0) for c in sc]
                m_scr[rows, :] = m0
                l_scr[rows, :] = functools.reduce(jnp.add, ps)
                acc_scr[rows, :] = functools.reduce(jnp.add, [p * vn_ref[j] for j, p in enumerate(ps)])

    qall = qall_scr[...]
    bias = bias_scr[...]
    scores = []
    for j in range(P):
        kj = k_refs[j][...].astype(BF16)
        s = lax.dot_general(qall, kj, (((1,), (1,)), ((), ())), preferred_element_type=F32)
        scores.append(s + bias)
    m_old = m_scr[...]
    m_new = m_old
    for s in scores:
        m_new = jnp.maximum(m_new, jnp.max(s, axis=1, keepdims=True))
    alpha = jnp.exp(m_old - m_new)
    l_new = alpha * l_scr[...]
    acc = alpha * acc_scr[...]
    for j in range(P):
        p = jnp.exp(scores[j] - m_new)
        l_new = l_new + jnp.sum(p, axis=1, keepdims=True)
        acc = acc + jnp.dot(p.astype(BF16), v_refs[j][...].astype(BF16), preferred_element_type=F32)
    m_scr[...] = m_new
    l_scr[...] = l_new
    acc_scr[...] = acc

    @pl.when(i == pl.num_programs(1) - 1)
    def _():
        lam = _lambda_value(lamv_ref, lam_init)
        o = _head_finish(acc_scr[...], l_scr[...], lam, g_ref[...], 1.0 - lam_init, S * H)
        o_ref[...] = o.reshape(S, H, E)


def _sample_attn(q, k_new, v_new, cache_k, cache_v, page_table, lamv, g, *, layer, pages_per_step, lam_init):
    DB, S, H, E = q.shape
    page_rows = cache_k.shape[2]
    n_pages = page_table.shape[1]
    P = pages_per_step
    R = 2 * S * H
    per_seq = pl.BlockSpec((None, S, H, E), lambda b, i, pt: (b, 0, 0, 0))

    def page_spec(j):
        return pl.BlockSpec((None, None, page_rows, E), lambda b, i, pt: (layer, pt[b, i * P + j], 0, 0))

    block_bytes = 2 * (2 * P * _nbytes((page_rows, E), F32)) + 2 * P * _nbytes((page_rows, E), BF16) \
        + (2 + P) * _nbytes((R, page_rows), F32)
    return pl.pallas_call(
        functools.partial(_sample_attn_kernel, pages_per_step=P, lam_init=lam_init),
        out_shape=jax.ShapeDtypeStruct((DB, S, H, E), F32),
        grid_spec=pltpu.PrefetchScalarGridSpec(
            num_scalar_prefetch=1,
            grid=(DB, n_pages // P),
            in_specs=[
                pl.BlockSpec(lamv.shape, lambda b, i, pt: (0, 0)),
                pl.BlockSpec((1, E), lambda b, i, pt: (0, 0)),
                per_seq, per_seq, per_seq,
            ] + [page_spec(j) for j in range(P)] + [page_spec(j) for j in range(P)],
            out_specs=per_seq,
            scratch_shapes=[
                pltpu.VMEM((R, E), BF16), pltpu.VMEM((R, page_rows), F32),
                pltpu.VMEM((R, 1), F32), pltpu.VMEM((R, 1), F32), pltpu.VMEM((R, E), F32),
            ],
        ),
        compiler_params=_compiler_params(("arbitrary", "arbitrary"), block_bytes),
        name="sample_attn",
    )(page_table, lamv, g.reshape(1, E), q, k_new, v_new, *([cache_k] * P), *([cache_v] * P))


def _conv_taps(u, gb, w_ref, prev1, prev2, t):
    sh1 = jnp.where(t >= 1, pltpu.roll(u, 1, axis=0), prev1)
    sh2 = jnp.where(t >= 2, pltpu.roll(u, 2, axis=0), prev2)
    return gb * (w_ref[0:1, :] * sh2 + w_ref[1:2, :] * sh1 + w_ref[2:3, :] * u)


def _prompt_conv_kernel(gb_ref, gc_ref, hc_ref, w_ref, y_ref, tail_ref, carry_scr, *, tiles_per_seq):
    i = pl.program_id(0)
    bt, C = gc_ref.shape
    u = gc_ref[...] * hc_ref[...]

    @pl.when(i % tiles_per_seq == 0)
    def _():
        carry_scr[...] = jnp.zeros(carry_scr.shape, F32)

    last = carry_scr[V7X_SUBLANES - 1:V7X_SUBLANES, :]
    second_last = carry_scr[V7X_SUBLANES - 2:V7X_SUBLANES - 1, :]
    t = lax.broadcasted_iota(jnp.int32, (bt, C), 0)
    prev1 = jnp.broadcast_to(last, (bt, C))
    prev2 = jnp.where(t == 0, jnp.broadcast_to(second_last, (bt, C)), prev1)
    y_ref[...] = _conv_taps(u, gb_ref[...], w_ref, prev1, prev2, t).astype(y_ref.dtype)
    tail = u[bt - V7X_SUBLANES:bt, :]
    carry_scr[...] = tail
    tail_ref[...] = tail


def _prompt_conv(gb, gc, hc, w, *, batch, bt):
    M, C = gc.shape
    tiles_per_seq = M // batch // bt
    rows = pl.BlockSpec((bt, C), lambda i: (i, 0))
    block_bytes = 2 * (3 * _nbytes((bt, C), F32) + _nbytes((bt, C), BF16)) + 6 * _nbytes((bt, C), F32)
    return pl.pallas_call(
        functools.partial(_prompt_conv_kernel, tiles_per_seq=tiles_per_seq),
        out_shape=[jax.ShapeDtypeStruct((M, C), BF16), jax.ShapeDtypeStruct((batch, V7X_SUBLANES, C), F32)],
        grid=(M // bt,),
        in_specs=[rows, rows, rows, pl.BlockSpec((CONV_K, C), lambda i: (0, 0))],
        out_specs=[rows, pl.BlockSpec((None, V7X_SUBLANES, C), lambda i: (i // tiles_per_seq, 0, 0))],
        scratch_shapes=[pltpu.VMEM((V7X_SUBLANES, C), F32)],
        compiler_params=_compiler_params(("arbitrary",), block_bytes),
        name="prompt_conv",
    )(gb, gc, hc, w)


def _sample_conv_kernel(gb_ref, gc_ref, hc_ref, w_ref, p1_ref, p2_ref, y_ref, u_ref, *, seq):
    u = gc_ref[...] * hc_ref[...]
    t = lax.broadcasted_iota(jnp.int32, u.shape, 0) % seq
    y_ref[...] = _conv_taps(u, gb_ref[...], w_ref, p1_ref[...], p2_ref[...], t).astype(y_ref.dtype)
    u_ref[...] = u


def _sample_conv(gb, gc, hc, w, state, *, seq):
    M, C = gc.shape
    DB = M // seq
    zeros = jnp.zeros((DB, seq - 1, C), F32)
    prev1 = jnp.concatenate([state[:, 1:2], zeros], axis=1).reshape(M, C)
    prev2 = jnp.concatenate([state, zeros[:, 1:]], axis=1).reshape(M, C)
    full = pl.BlockSpec((M, C), lambda: (0, 0))
    return pl.pallas_call(
        functools.partial(_sample_conv_kernel, seq=seq),
        out_shape=[jax.ShapeDtypeStruct((M, C), BF16), jax.ShapeDtypeStruct((M, C), F32)],
        in_specs=[full, full, full, pl.BlockSpec((CONV_K, C), lambda: (0, 0)), full, full],
        out_specs=[full, full],
        name="sample_conv",
    )(gb, gc, hc, w, prev1, prev2)


def _out_proj_kernel(a_ref, y_ref, w_ref, r_ref, o_ref):
    ka = a_ref.shape[1]
    acc = jnp.dot(a_ref[...], w_ref[0:ka, :], preferred_element_type=F32)
    acc = acc + jnp.dot(y_ref[...], w_ref[ka:, :], preferred_element_type=F32)
    o_ref[...] = r_ref[...] + acc


def _out_proj(a, y, w, res, *, bm, bn):
    M, ka = a.shape
    ky = y.shape[1]
    N = w.shape[1]
    block_bytes = 2 * (_nbytes((bm, ka + ky), BF16) + _nbytes((ka + ky, bn), BF16) + 2 * _nbytes((bm, bn), F32))
    return pl.pallas_call(
        _out_proj_kernel,
        out_shape=jax.ShapeDtypeStruct((M, N), F32),
        grid=(M // bm, N // bn),
        in_specs=[
            pl.BlockSpec((bm, ka), lambda m, n: (m, 0)),
            pl.BlockSpec((bm, ky), lambda m, n: (m, 0)),
            pl.BlockSpec((ka + ky, bn), lambda m, n: (0, n)),
            pl.BlockSpec((bm, bn), lambda m, n: (m, n)),
        ],
        out_specs=pl.BlockSpec((bm, bn), lambda m, n: (m, n)),
        compiler_params=_compiler_params(("arbitrary", "arbitrary"), block_bytes),
        name="out_proj",
    )(a, y, w, res)


def _gate_up_kernel(x_ref, g_ref, wg_ref, wu_ref, o_ref, h_scr):
    @pl.when(pl.program_id(1) == 0)
    def _():
        h_scr[...] = (_rms_scale(x_ref[...], RMS_EPS) * g_ref[...]).astype(BF16)

    h = h_scr[...]
    gate = jnp.dot(h, wg_ref[...], preferred_element_type=F32)
    up = jnp.dot(h, wu_ref[...], preferred_element_type=F32)
    o_ref[...] = ((gate / (1.0 + jnp.exp(-gate))) * up).astype(o_ref.dtype)


def _gate_up(x, g, wg, wu, *, bm, bn):
    M, D = x.shape
    F = wg.shape[1]
    block_bytes = 2 * (_nbytes((bm, D), F32) + 2 * _nbytes((D, bn), BF16) + _nbytes((bm, bn), BF16)) \
        + _nbytes((bm, D), BF16)
    return pl.pallas_call(
        _gate_up_kernel,
        out_shape=jax.ShapeDtypeStruct((M, F), BF16),
        grid=(M // bm, F // bn),
        in_specs=[
            pl.BlockSpec((bm, D), lambda m, n: (m, 0)),
            pl.BlockSpec((1, D), lambda m, n: (0, 0)),
            pl.BlockSpec((D, bn), lambda m, n: (0, n)),
            pl.BlockSpec((D, bn), lambda m, n: (0, n)),
        ],
        out_specs=pl.BlockSpec((bm, bn), lambda m, n: (m, n)),
        scratch_shapes=[pltpu.VMEM((bm, D), BF16)],
        compiler_params=_compiler_params(("arbitrary", "arbitrary"), block_bytes),
        name="gate_up",
    )(x, g.reshape(1, D), wg, wu)


def _down_kernel(a_ref, w_ref, r_ref, o_ref):
    o_ref[...] = r_ref[...] + jnp.dot(a_ref[...], w_ref[...], preferred_element_type=F32)


def _down(a, w, res, *, bm, bn):
    M, F = a.shape
    N = w.shape[1]
    block_bytes = 2 * (_nbytes((bm, F), BF16) + _nbytes((F, bn), BF16) + 2 * _nbytes((bm, bn), F32))
    return pl.pallas_call(
        _down_kernel,
        out_shape=jax.ShapeDtypeStruct((M, N), F32),
        grid=(M // bm, N // bn),
        in_specs=[
            pl.BlockSpec((bm, F), lambda m, n: (m, 0)),
            pl.BlockSpec((F, bn), lambda m, n: (0, n)),
            pl.BlockSpec((bm, bn), lambda m, n: (m, n)),
        ],
        out_specs=pl.BlockSpec((bm, bn), lambda m, n: (m, n)),
        compiler_params=_compiler_params(("arbitrary", "arbitrary"), block_bytes),
        name="down_proj",
    )(a, w, res)


def _final_norm_kernel(x_ref, g_ref, o_ref):
    o_ref[...] = _rms_scale(x_ref[...], RMS_EPS) * g_ref[...]


def _final_norm(x, g, *, bm):
    M, D = x.shape
    rows = pl.BlockSpec((bm, D), lambda m: (m, 0))
    return pl.pallas_call(
        _final_norm_kernel,
        out_shape=jax.ShapeDtypeStruct((M, D), F32),
        grid=(M // bm,),
        in_specs=[rows, pl.BlockSpec((1, D), lambda m: (0, 0))],
        out_specs=rows,
        compiler_params=_compiler_params(("arbitrary",), 4 * _nbytes((bm, D), F32)),
        name="final_norm",
    )(x, g.reshape(1, D))


def _pick_block(size, preferred):
    b = min(size, preferred)
    while size % b:
        b //= 2
    return b


def kernel(x_prompt, x_sample, cache_k, cache_v, state_conv, page_table, attn_norm, w_in, conv_w,
           lambda_q1, lambda_k1, lambda_q2, lambda_k2, subln_g, w_out, ffn_norm, w_gate, w_up,
           w_down, final_norm):
    B, T, D = x_prompt.shape
    DB, S, _ = x_sample.shape
    depth, pool, page, H, E = cache_k.shape
    assert E == V7X_LANES and H == V7X_SUBLANES, "one (head, value) tile per position is assumed"
    head_dim = E // 2
    W = H * E
    C = conv_w.shape[2]
    past = page_table.shape[1] * page
    Mp, Ms = B * T, DB * S

    cos_p, sin_p = _rope_tables(np.arange(T), head_dim)
    cos_s, sin_s = _rope_tables(np.tile(past + np.arange(S), DB), head_dim)
    ck = cache_k.reshape(depth, pool, page * H, E)
    cv = cache_v.reshape(depth, pool, page * H, E)

    bm_in = _pick_block(T, 512)
    bm = _pick_block(T, 1024)
    blk = _pick_block(T, 512)
    bt = _pick_block(T, 512)
    bn_ff = _pick_block(w_gate.shape[2], 512)
    pages_per_step = _pick_block(page_table.shape[1], 8)

    xp = x_prompt.reshape(Mp, D)
    xs = x_sample.reshape(Ms, D)
    kp_l, vp_l, cp_l, ks_l, vs_l, cs_l = [], [], [], [], [], []
    for l in range(depth):
        lam_init = 0.8 - 0.6 * math.exp(-0.3 * l)
        lamv = jnp.stack([lambda_q1[l], lambda_k1[l], lambda_q2[l], lambda_k2[l]])
        wi = w_in[l].astype(BF16)
        wo = w_out[l].astype(BF16)
        wg = w_gate[l].astype(BF16)
        wu = w_up[l].astype(BF16)
        wd = w_down[l].astype(BF16)

        q, kb, vb, k, v, gb, gc, hc = _in_proj(xp, attn_norm[l], wi, cos_p, sin_p, bm=bm_in, n_heads=H)
        ao = _prompt_attn(q, kb, vb, lamv, subln_g[l], batch=B, blk=blk, lam_init=lam_init)
        yc, tail = _prompt_conv(gb, gc, hc, conv_w[l], batch=B, bt=bt)
        xp = _out_proj(ao, yc, wo, xp, bm=bm, bn=1024)
        act = _gate_up(xp, ffn_norm[l], wg, wu, bm=bm, bn=bn_ff)
        xp = _down(act, wd, xp, bm=bm, bn=512)
        kp_l.append(k.reshape(B, T, H, E))
        vp_l.append(v.reshape(B, T, H, E))
        cp_l.append(tail[:, V7X_SUBLANES - (CONV_K - 1):])

        q, _, _, k, v, gb, gc, hc = _in_proj(xs, attn_norm[l], wi, cos_s, sin_s, bm=Ms, n_heads=H)
        ao = _sample_attn(q.astype(F32).reshape(DB, S, H, E), k.reshape(DB, S, H, E), v.reshape(DB, S, H, E),
                          ck, cv, page_table, lamv, subln_g[l], layer=l, pages_per_step=pages_per_step,
                          lam_init=lam_init)
        yc, u = _sample_conv(gb, gc, hc, conv_w[l], state_conv[l], seq=S)
        xs = _out_proj(ao.reshape(Ms, W).astype(BF16), yc, wo, xs, bm=Ms, bn=1024)
        act = _gate_up(xs, ffn_norm[l], wg, wu, bm=Ms, bn=bn_ff)
        xs = _down(act, wd, xs, bm=Ms, bn=512)
        ks_l.append(k.reshape(DB, S, H, E))
        vs_l.append(v.reshape(DB, S, H, E))
        cs_l.append(u.reshape(DB, S, C)[:, S - (CONV_K - 1):])

    y_prompt = _final_norm(xp, final_norm, bm=bm).reshape(B, T, D)
    y_sample = _final_norm(xs, final_norm, bm=Ms).reshape(DB, S, D)
    return (y_prompt, y_sample, jnp.stack(kp_l), jnp.stack(vp_l), jnp.stack(cp_l),
            jnp.stack(ks_l), jnp.stack(vs_l), jnp.stack(cs_l))
```

```python
import functools
import math

import numpy as np
import jax
import jax.numpy as jnp
from jax import lax
from jax.experimental import pallas as pl
from jax.experimental.pallas import tpu as pltpu

ROPE_THETA = 10000.0
RMS_EPS = 1e-6
SUBLN_EPS = 1e-5
CONV_K = 3
MASK_VALUE = -1e30
LOG2_E = math.log2(math.e)

V7X_LANES = 128
V7X_SUBLANES = 8
V7X_VMEM_BYTES = 64 * 1024 * 1024
VMEM_TEMP_BYTES = 12 * 1024 * 1024

F32 = jnp.float32
BF16 = jnp.bfloat16
NT_DIMS = (((1,), (1,)), ((), ()))


def _compiler_params(semantics, block_bytes):
    limit = min(block_bytes + VMEM_TEMP_BYTES, V7X_VMEM_BYTES - 6 * 1024 * 1024)
    return pltpu.CompilerParams(dimension_semantics=semantics, vmem_limit_bytes=int(limit))


def _nbytes(shape, dtype):
    return int(np.prod(shape)) * jnp.dtype(dtype).itemsize


def _rms_scale(x, eps):
    return x * lax.rsqrt(jnp.mean(x * x, axis=-1, keepdims=True) + eps)


def _row_rsqrt(ssq_ref, width):
    return lax.rsqrt(jnp.sum(ssq_ref[...], axis=1, keepdims=True) * (1.0 / width) + RMS_EPS)


def _lane_fold(x, op):
    return functools.reduce(op, [x[:, j * V7X_LANES:(j + 1) * V7X_LANES] for j in range(x.shape[1] // V7X_LANES)])


def _emit_norm_inputs(x, g_ref, xg_ref, ssq_ref):
    xg_ref[...] = (x * g_ref[...]).astype(BF16)
    ssq_ref[...] = _lane_fold(x * x, jnp.add)


def _lambda_value(lamv_ref, lam_init):
    d1 = jnp.sum(lamv_ref[0:1, :] * lamv_ref[1:2, :], axis=1, keepdims=True)
    d2 = jnp.sum(lamv_ref[2:3, :] * lamv_ref[3:4, :], axis=1, keepdims=True)
    return jnp.exp(d1) - jnp.exp(d2) + lam_init


def _head_finish(acc, l, lam, g, out_scale, rows):
    o = acc[0:rows] / l[0:rows] - lam * (acc[rows:2 * rows] / l[rows:2 * rows])
    return (_rms_scale(o, SUBLN_EPS) * g) * out_scale


def _prep_kernel(x_ref, g_ref, xg_ref, ssq_ref):
    _emit_norm_inputs(x_ref[...], g_ref, xg_ref, ssq_ref)


def _prep(x, g, *, bm):
    M, D = x.shape
    rows = lambda width: pl.BlockSpec((bm, width), lambda m: (m, 0))
    return pl.pallas_call(
        _prep_kernel,
        out_shape=[jax.ShapeDtypeStruct((M, D), BF16), jax.ShapeDtypeStruct((M, V7X_LANES), F32)],
        grid=(M // bm,),
        in_specs=[rows(D), pl.BlockSpec((1, D), lambda m: (0, 0))],
        out_specs=[rows(D), rows(V7X_LANES)],
        compiler_params=_compiler_params(("arbitrary",), 3 * _nbytes((bm, D), F32)),
        name="prep",
    )(x, g.reshape(1, D))


def _rope_tables(positions, head_dim):
    inv = 1.0 / (ROPE_THETA ** (np.arange(0, head_dim, 2, dtype=np.float64) / head_dim))
    ang = np.asarray(positions, np.float64)[:, None] * inv[None, :]
    ang = np.concatenate([ang, ang], axis=-1)
    sign = np.concatenate([-np.ones(head_dim // 2), np.ones(head_dim // 2)])
    reps = V7X_LANES // head_dim
    cos = np.tile(np.cos(ang), (1, reps)).astype(np.float32)
    sin = np.tile(np.sin(ang) * sign[None, :], (1, reps)).astype(np.float32)
    return jnp.asarray(cos), jnp.asarray(sin)


def _rope(zc, cos, sin, low, half):
    ahead = pltpu.roll(zc, V7X_LANES - half, axis=1)
    behind = pltpu.roll(zc, half, axis=1)
    return zc * cos + jnp.where(low, ahead, behind) * sin


def _attn_proj_kernel(xg_ref, ssq_ref, w_ref, cos_ref, sin_ref, *refs, half, q_scale):
    qkv_ref, k_ref, v_ref, w_scr = refs[-4:]
    n = pl.program_id(0)
    bm, W = qkv_ref.shape
    E = k_ref.shape[1]
    H = W // E

    @pl.when(pl.program_id(1) == 0)
    def _():
        w_scr[...] = w_ref[...].astype(BF16)

    r = _row_rsqrt(ssq_ref, xg_ref.shape[1])
    z = jnp.dot(xg_ref[...], w_scr[...], preferred_element_type=F32) * r
    lane = lax.broadcasted_iota(jnp.int32, (bm, E), 1)
    low = (lane % (2 * half)) < half

    @pl.when(n == 0)
    def _():
        for h in range(H):
            q = _rope(z[:, h * E:(h + 1) * E], cos_ref[...], sin_ref[...], low, half)
            qkv_ref[:, h * E:(h + 1) * E] = (q * q_scale).astype(BF16)

    @pl.when(n == 1)
    def _():
        for h in range(H):
            k = _rope(z[:, h * E:(h + 1) * E], cos_ref[...], sin_ref[...], low, half)
            qkv_ref[:, h * E:(h + 1) * E] = k.astype(BF16)
            k_ref[pl.ds(h, bm, stride=H), :] = k

    @pl.when(n == 2)
    def _():
        qkv_ref[...] = z.astype(BF16)
        for h in range(H):
            v_ref[pl.ds(h, bm, stride=H), :] = z[:, h * E:(h + 1) * E]


def _attn_proj(xg, ssq, w_in, cos, sin, kv_all, *, layer, slab, n_slabs, bm, n_heads):
    M, D = xg.shape
    W = w_in.shape[2] // 6
    H = n_heads
    E = W // H
    head_dim = E // 2
    n_row_tiles = M // bm
    n_pos_blocks = cos.shape[0] // bm
    k_rows = lambda n, m: (slab, jnp.where(n == 1, m, jnp.where(n < 1, 0, n_row_tiles - 1)), 0)
    v_rows = lambda n, m: (slab, jnp.where(n == 2, m, 0), 0)
    stacked = jax.ShapeDtypeStruct((n_slabs, M * H, E), F32)
    aliases = {} if kv_all is None else {5: 1, 6: 2}
    extra_in = [] if kv_all is None else [pl.BlockSpec(memory_space=pl.ANY)] * 2
    block_bytes = 2 * (_nbytes((bm, D), BF16) + _nbytes((D, W), F32) + _nbytes((bm, W), BF16)
                       + 2 * _nbytes((bm, W), F32)) + _nbytes((D, W), BF16) + _nbytes((bm, W), F32)
    return pl.pallas_call(
        functools.partial(_attn_proj_kernel, half=head_dim // 2, q_scale=head_dim ** -0.5 * LOG2_E),
        out_shape=[jax.ShapeDtypeStruct((M, 3 * W), BF16), stacked, stacked],
        grid=(3, n_row_tiles),
        in_specs=[
            pl.BlockSpec((bm, D), lambda n, m: (m, 0)),
            pl.BlockSpec((bm, ssq.shape[1]), lambda n, m: (m, 0)),
            pl.BlockSpec((None, D, W), lambda n, m: (layer, 0, n)),
            pl.BlockSpec((bm, V7X_LANES), lambda n, m: (m % n_pos_blocks, 0)),
            pl.BlockSpec((bm, V7X_LANES), lambda n, m: (m % n_pos_blocks, 0)),
        ] + extra_in,
        out_specs=[
            pl.BlockSpec((bm, W), lambda n, m: (m, n)),
            pl.BlockSpec((None, bm * H, E), k_rows),
            pl.BlockSpec((None, bm * H, E), v_rows),
        ],
        scratch_shapes=[pltpu.VMEM((D, W), BF16)],
        input_output_aliases=aliases,
        compiler_params=_compiler_params(("arbitrary", "arbitrary"), block_bytes),
        name="attn_proj",
    )(xg, ssq, w_in, cos, sin, *([] if kv_all is None else kv_all))


def _conv_proj_kernel(xg_ref, ssq_ref, wb_ref, wc_ref, wh_ref, cw_ref, *refs, seq, tiles_per_seq):
    carried = tiles_per_seq is not None
    if carried:
        y_ref, tail_ref, wb_scr, wc_scr, wh_scr, carry_scr = refs
    else:
        prev1_ref, prev2_ref, y_ref, tail_ref, wb_scr, wc_scr, wh_scr = refs
    m = pl.program_id(1)
    bm, bc = y_ref.shape

    @pl.when(m == 0)
    def _():
        wb_scr[...] = wb_ref[...].astype(BF16)
        wc_scr[...] = wc_ref[...].astype(BF16)
        wh_scr[...] = wh_ref[...].astype(BF16)

    r = _row_rsqrt(ssq_ref, xg_ref.shape[1])
    xg = xg_ref[...]
    gb = jnp.dot(xg, wb_scr[...], preferred_element_type=F32) * r
    gc = jnp.dot(xg, wc_scr[...], preferred_element_type=F32) * r
    hc = jnp.dot(xg, wh_scr[...], preferred_element_type=F32) * r
    u = gc * hc
    t = lax.broadcasted_iota(jnp.int32, (bm, bc), 0)
    if carried:
        @pl.when(m % tiles_per_seq == 0)
        def _():
            carry_scr[...] = jnp.zeros(carry_scr.shape, F32)

        last = jnp.broadcast_to(carry_scr[V7X_SUBLANES - 1:V7X_SUBLANES, :], (bm, bc))
        second_last = jnp.broadcast_to(carry_scr[V7X_SUBLANES - 2:V7X_SUBLANES - 1, :], (bm, bc))
        prev1 = last
        prev2 = jnp.where(t == 0, second_last, last)
    else:
        t = t % seq
        prev1 = prev1_ref[...]
        prev2 = prev2_ref[...]
    sh1 = jnp.where(t >= 1, pltpu.roll(u, 1, axis=0), prev1)
    sh2 = jnp.where(t >= 2, pltpu.roll(u, 2, axis=0), prev2)
    y = gb * (cw_ref[0:1, :] * sh2 + cw_ref[1:2, :] * sh1 + cw_ref[2:3, :] * u)
    y_ref[...] = y.astype(BF16)
    if carried:
        tail = u[bm - V7X_SUBLANES:bm, :]
        carry_scr[...] = tail
        tail_ref[...] = tail
    else:
        tail_ref[...] = u


def _conv_proj(xg, ssq, w_in, conv_w, prev, *, layer, bm, bc, seq):
    M, D = xg.shape
    C = conv_w.shape[2]
    col0 = (w_in.shape[2] - 3 * C) // bc
    w_spec = lambda j: pl.BlockSpec((None, D, bc), lambda c, m: (layer, 0, col0 + j * (C // bc) + c))
    tile = pl.BlockSpec((bm, bc), lambda c, m: (m, c))
    in_specs = [
        pl.BlockSpec((bm, D), lambda c, m: (m, 0)),
        pl.BlockSpec((bm, ssq.shape[1]), lambda c, m: (m, 0)),
        w_spec(0), w_spec(1), w_spec(2),
        pl.BlockSpec((None, CONV_K, bc), lambda c, m: (layer, 0, c)),
    ]
    scratch = [pltpu.VMEM((D, bc), BF16)] * 3
    if prev is None:
        tiles_per_seq = seq // bm
        batch = M // seq
        tail_shape = jax.ShapeDtypeStruct((batch, V7X_SUBLANES, C), F32)
        tail_spec = pl.BlockSpec((None, V7X_SUBLANES, bc), lambda c, m: (m // tiles_per_seq, 0, c))
        scratch = scratch + [pltpu.VMEM((V7X_SUBLANES, bc), F32)]
        operands = ()
    else:
        tiles_per_seq = None
        tail_shape = jax.ShapeDtypeStruct((M, C), F32)
        tail_spec = tile
        in_specs = in_specs + [tile, tile]
        operands = prev
    block_bytes = 2 * (_nbytes((bm, D), BF16) + 3 * _nbytes((D, bc), F32) + 4 * _nbytes((bm, bc), F32)) \
        + 3 * _nbytes((D, bc), BF16) + 6 * _nbytes((bm, bc), F32)
    return pl.pallas_call(
        functools.partial(_conv_proj_kernel, seq=seq, tiles_per_seq=tiles_per_seq),
        out_shape=[jax.ShapeDtypeStruct((M, C), BF16), tail_shape],
        grid=(C // bc, M // bm),
        in_specs=in_specs,
        out_specs=[tile, tail_spec],
        scratch_shapes=scratch,
        compiler_params=_compiler_params(("arbitrary", "arbitrary"), block_bytes),
        name="conv_proj",
    )(xg, ssq, w_in, w_in, w_in, conv_w, *operands)


def _sample_conv_prev(state, seq):
    DB, _, C = state.shape
    zeros = jnp.zeros((DB, seq - 1, C), F32)
    prev1 = jnp.concatenate([state[:, 1:2], zeros], axis=1).reshape(DB * seq, C)
    prev2 = jnp.concatenate([state, zeros[:, 1:]], axis=1).reshape(DB * seq, C)
    return prev1, prev2


def _prompt_attn_kernel(lamv_ref, g_ref, q_ref, k_ref, v_ref, o_ref, qs_scr, p_scr, *, blk, lam_init):
    T, E = q_ref.shape
    half = E // 2
    lam = _lambda_value(lamv_ref, lam_init)
    lane = lax.broadcasted_iota(jnp.int32, (blk, E), 1)
    row = lax.broadcasted_iota(jnp.int32, (2 * blk, blk), 0)
    col = lax.broadcasted_iota(jnp.int32, (2 * blk, blk), 1)
    causal = col <= jnp.where(row >= blk, row - blk, row)

    def scores(c, diagonal):
        s = lax.dot_general(qs_scr[...], k_ref[c * blk:(c + 1) * blk, :], NT_DIMS, preferred_element_type=F32)
        return jnp.where(causal, s, MASK_VALUE) if diagonal else s

    for qi in range(T // blk):
        q = q_ref[qi * blk:(qi + 1) * blk, :]
        qs_scr[0:blk, :] = jnp.where(lane < half, q, jnp.zeros_like(q))
        qs_scr[blk:2 * blk, :] = jnp.where(lane >= half, q, jnp.zeros_like(q))
        m_lanes = jnp.full((2 * blk, V7X_LANES), MASK_VALUE, F32)
        for c in range(qi + 1):
            m_lanes = jnp.maximum(m_lanes, _lane_fold(scores(c, c == qi), jnp.maximum))
        m = jnp.broadcast_to(jnp.max(m_lanes, axis=1, keepdims=True), (2 * blk, V7X_LANES))
        l_lanes = jnp.zeros((2 * blk, V7X_LANES), F32)
        for c in range(qi + 1):
            s = scores(c, c == qi)
            for j in range(blk // V7X_LANES):
                p = jnp.exp2(s[:, j * V7X_LANES:(j + 1) * V7X_LANES] - m)
                l_lanes = l_lanes + p
                p_scr[:, c * blk + j * V7X_LANES:c * blk + (j + 1) * V7X_LANES] = p.astype(BF16)
        l = jnp.sum(l_lanes, axis=1, keepdims=True)
        visible = (qi + 1) * blk
        acc = jnp.dot(p_scr[:, 0:visible], v_ref[0:visible, :], preferred_element_type=F32)
        o = _head_finish(acc, l, lam, g_ref[...], 1.0 - lam_init, blk)
        o_ref[qi * blk:(qi + 1) * blk, :] = o.astype(o_ref.dtype)


def _prompt_attn(qkv, lamv, g, *, batch, blk, lam_init):
    M = qkv.shape[0]
    E = g.shape[0]
    H = qkv.shape[1] // (3 * E)
    T = M // batch
    block_bytes = 2 * 4 * _nbytes((T, E), BF16) + _nbytes((2 * blk, E), BF16) + _nbytes((2 * blk, T), BF16) \
        + 3 * _nbytes((2 * blk, blk), F32)
    return pl.pallas_call(
        functools.partial(_prompt_attn_kernel, blk=blk, lam_init=lam_init),
        out_shape=jax.ShapeDtypeStruct((M, H * E), BF16),
        grid=(batch, H),
        in_specs=[
            pl.BlockSpec(lamv.shape, lambda b, h: (0, 0)),
            pl.BlockSpec((1, E), lambda b, h: (0, 0)),
            pl.BlockSpec((T, E), lambda b, h: (b, h)),
            pl.BlockSpec((T, E), lambda b, h: (b, H + h)),
            pl.BlockSpec((T, E), lambda b, h: (b, 2 * H + h)),
        ],
        out_specs=pl.BlockSpec((T, E), lambda b, h: (b, h)),
        scratch_shapes=[pltpu.VMEM((2 * blk, E), BF16), pltpu.VMEM((2 * blk, T), BF16)],
        compiler_params=_compiler_params(("arbitrary", "arbitrary"), block_bytes),
        name="prompt_attn",
    )(lamv, g.reshape(1, E), qkv, qkv, qkv)


def _sample_attn_kernel(pt_ref, lamv_ref, g_ref, q_ref, kn_ref, vn_ref, *refs,
                        pages_per_step, lam_init):
    P = pages_per_step
    k_refs, v_refs = refs[:P], refs[P:2 * P]
    o_ref, qall_scr, bias_scr, m_scr, l_scr, acc_scr = refs[2 * P:]
    S, H, E = q_ref.shape
    half = E // 2
    i = pl.program_id(1)

    @pl.when(i == 0)
    def _():
        rr = lax.broadcasted_iota(jnp.int32, bias_scr.shape, 0)
        cc = lax.broadcasted_iota(jnp.int32, bias_scr.shape, 1)
        bias_scr[...] = jnp.where(cc % H == rr % H, 0.0, MASK_VALUE)
        lane = lax.broadcasted_iota(jnp.int32, (H, E), 1)
        for m in range(2):
            for s in range(S):
                rows = pl.ds((m * S + s) * H, H)
                q = jnp.where(lane // half == m, q_ref[s], 0.0)
                qall_scr[rows, :] = q.astype(BF16)
                sc = [jnp.sum(q * kn_ref[j], axis=1, keepdims=True) for j in range(s + 1)]
                m0 = functools.reduce(jnp.maximum, sc)
                ps = [jnp.exp2(c - m0) for c in sc]
                m_scr[rows, :] = m0
                l_scr[rows, :] = functools.reduce(jnp.add, ps)
                acc_scr[rows, :] = functools.reduce(jnp.add, [p * vn_ref[j] for j, p in enumerate(ps)])

    qall = qall_scr[...]
    bias = bias_scr[...]
    scores = []
    for j in range(P):
        kj = k_refs[j][...].astype(BF16)
        scores.append(lax.dot_general(qall, kj, NT_DIMS, preferred_element_type=F32) + bias)
    m_old = m_scr[...]
    m_new = m_old
    for s in scores:
        m_new = jnp.maximum(m_new, jnp.max(s, axis=1, keepdims=True))
    alpha = jnp.exp2(m_old - m_new)
    l_new = alpha * l_scr[...]
    acc = alpha * acc_scr[...]
    for j in range(P):
        p = jnp.exp2(scores[j] - m_new)
        l_new = l_new + jnp.sum(p, axis=1, keepdims=True)
        acc = acc + jnp.dot(p.astype(BF16), v_refs[j][...].astype(BF16), preferred_element_type=F32)
    m_scr[...] = m_new
    l_scr[...] = l_new
    acc_scr[...] = acc

    @pl.when(i == pl.num_programs(1) - 1)
    def _():
        lam = _lambda_value(lamv_ref, lam_init)
        o = _head_finish(acc_scr[...], l_scr[...], lam, g_ref[...], 1.0 - lam_init, S * H)
        o_ref[...] = o.reshape(S, H, E)


def _sample_attn(q, k_new, v_new, cache_k, cache_v, page_table, lamv, g, *, layer, pages_per_step, lam_init):
    DB, S, H, E = q.shape
    page_rows = cache_k.shape[2]
    n_pages = page_table.shape[1]
    P = pages_per_step
    R = 2 * S * H
    per_seq = pl.BlockSpec((None, S, H, E), lambda b, i, pt: (b, 0, 0, 0))

    def page_spec(j):
        return pl.BlockSpec((None, None, page_rows, E), lambda b, i, pt: (layer, pt[b, i * P + j], 0, 0))

    block_bytes = 2 * (2 * P * _nbytes((page_rows, E), F32)) + 2 * P * _nbytes((page_rows, E), BF16) \
        + (2 + P) * _nbytes((R, page_rows), F32)
    return pl.pallas_call(
        functools.partial(_sample_attn_kernel, pages_per_step=P, lam_init=lam_init),
        out_shape=jax.ShapeDtypeStruct((DB, S, H, E), F32),
        grid_spec=pltpu.PrefetchScalarGridSpec(
            num_scalar_prefetch=1,
            grid=(DB, n_pages // P),
            in_specs=[
                pl.BlockSpec(lamv.shape, lambda b, i, pt: (0, 0)),
                pl.BlockSpec((1, E), lambda b, i, pt: (0, 0)),
                per_seq, per_seq, per_seq,
            ] + [page_spec(j) for j in range(P)] + [page_spec(j) for j in range(P)],
            out_specs=per_seq,
            scratch_shapes=[
                pltpu.VMEM((R, E), BF16), pltpu.VMEM((R, page_rows), F32),
                pltpu.VMEM((R, 1), F32), pltpu.VMEM((R, 1), F32), pltpu.VMEM((R, E), F32),
            ],
        ),
        compiler_params=_compiler_params(("arbitrary", "arbitrary"), block_bytes),
        name="sample_attn",
    )(page_table, lamv, g.reshape(1, E), q, k_new, v_new, *([cache_k] * P), *([cache_v] * P))


def _out_proj_kernel(a_ref, y_ref, w_ref, r_ref, g_ref, x_ref, xg_ref, ssq_ref, w_scr):
    @pl.when(pl.program_id(1) == 0)
    def _():
        w_scr[...] = w_ref[...].astype(BF16)

    ka = a_ref.shape[1]
    acc = jnp.dot(a_ref[...], w_scr[0:ka, :], preferred_element_type=F32)
    acc = acc + jnp.dot(y_ref[...], w_scr[ka:, :], preferred_element_type=F32)
    x = r_ref[...] + acc
    x_ref[...] = x
    _emit_norm_inputs(x, g_ref, xg_ref, ssq_ref)


def _out_proj(a, y, w, res, g, *, layer, bm, bn):
    M, ka = a.shape
    K = ka + y.shape[1]
    N = w.shape[2]
    tile = lambda dtype_width: pl.BlockSpec((bm, dtype_width), lambda n, m: (m, n))
    block_bytes = 2 * (_nbytes((bm, K), BF16) + _nbytes((K, bn), F32) + 3 * _nbytes((bm, bn), F32)) \
        + _nbytes((K, bn), BF16) + 2 * _nbytes((bm, bn), F32)
    return pl.pallas_call(
        _out_proj_kernel,
        out_shape=[jax.ShapeDtypeStruct((M, N), F32), jax.ShapeDtypeStruct((M, N), BF16),
                   jax.ShapeDtypeStruct((M, N // bn * V7X_LANES), F32)],
        grid=(N // bn, M // bm),
        in_specs=[
            pl.BlockSpec((bm, ka), lambda n, m: (m, 0)),
            pl.BlockSpec((bm, K - ka), lambda n, m: (m, 0)),
            pl.BlockSpec((None, K, bn), lambda n, m: (layer, 0, n)),
            tile(bn),
            pl.BlockSpec((1, bn), lambda n, m: (0, n)),
        ],
        out_specs=[tile(bn), tile(bn), tile(V7X_LANES)],
        scratch_shapes=[pltpu.VMEM((K, bn), BF16)],
        compiler_params=_compiler_params(("arbitrary", "arbitrary"), block_bytes),
        name="out_proj",
    )(a, y, w, res, g.reshape(1, N))


def _gate_up_kernel(xg_ref, ssq_ref, wg_ref, wu_ref, o_ref, wg_scr, wu_scr):
    @pl.when(pl.program_id(1) == 0)
    def _():
        wg_scr[...] = wg_ref[...].astype(BF16)
        wu_scr[...] = wu_ref[...].astype(BF16)

    r = _row_rsqrt(ssq_ref, xg_ref.shape[1])
    xg = xg_ref[...]
    gate = jnp.dot(xg, wg_scr[...], preferred_element_type=F32) * r
    up = jnp.dot(xg, wu_scr[...], preferred_element_type=F32) * r
    o_ref[...] = ((gate / (1.0 + jnp.exp(-gate))) * up).astype(o_ref.dtype)


def _gate_up(xg, ssq, wg, wu, *, layer, bm, bn):
    M, D = xg.shape
    F = wg.shape[2]
    w_spec = pl.BlockSpec((None, D, bn), lambda n, m: (layer, 0, n))
    block_bytes = 2 * (_nbytes((bm, D), BF16) + 2 * _nbytes((D, bn), F32) + _nbytes((bm, bn), BF16)) \
        + 2 * _nbytes((D, bn), BF16) + 3 * _nbytes((bm, bn), F32)
    return pl.pallas_call(
        _gate_up_kernel,
        out_shape=jax.ShapeDtypeStruct((M, F), BF16),
        grid=(F // bn, M // bm),
        in_specs=[
            pl.BlockSpec((bm, D), lambda n, m: (m, 0)),
            pl.BlockSpec((bm, ssq.shape[1]), lambda n, m: (m, 0)),
            w_spec, w_spec,
        ],
        out_specs=pl.BlockSpec((bm, bn), lambda n, m: (m, n)),
        scratch_shapes=[pltpu.VMEM((D, bn), BF16)] * 2,
        compiler_params=_compiler_params(("arbitrary", "arbitrary"), block_bytes),
        name="gate_up",
    )(xg, ssq, wg, wu)


def _down_kernel(a_ref, w_ref, r_ref, g_ref, x_ref, xg_ref, ssq_ref, w_scr):
    @pl.when(pl.program_id(1) == 0)
    def _():
        w_scr[...] = w_ref[...].astype(BF16)

    x = r_ref[...] + jnp.dot(a_ref[...], w_scr[...], preferred_element_type=F32)
    x_ref[...] = x
    _emit_norm_inputs(x, g_ref, xg_ref, ssq_ref)


def _down(a, w, res, g, *, layer, bm, bn):
    M, F = a.shape
    N = w.shape[2]
    tile = lambda width: pl.BlockSpec((bm, width), lambda n, m: (m, n))
    block_bytes = 2 * (_nbytes((bm, F), BF16) + _nbytes((F, bn), F32) + 3 * _nbytes((bm, bn), F32)) \
        + _nbytes((F, bn), BF16) + 2 * _nbytes((bm, bn), F32)
    return pl.pallas_call(
        _down_kernel,
        out_shape=[jax.ShapeDtypeStruct((M, N), F32), jax.ShapeDtypeStruct((M, N), BF16),
                   jax.ShapeDtypeStruct((M, N // bn * V7X_LANES), F32)],
        grid=(N // bn, M // bm),
        in_specs=[
            pl.BlockSpec((bm, F), lambda n, m: (m, 0)),
            pl.BlockSpec((None, F, bn), lambda n, m: (layer, 0, n)),
            tile(bn),
            pl.BlockSpec((1, bn), lambda n, m: (0, n)),
        ],
        out_specs=[tile(bn), tile(bn), tile(V7X_LANES)],
        scratch_shapes=[pltpu.VMEM((F, bn), BF16)],
        compiler_params=_compiler_params(("arbitrary", "arbitrary"), block_bytes),
        name="down_proj",
    )(a, w, res, g.reshape(1, N))


def _final_norm_kernel(x_ref, g_ref, o_ref):
    o_ref[...] = _rms_scale(x_ref[...], RMS_EPS) * g_ref[...]


def _final_norm(x, g, *, bm):
    M, D = x.shape
    rows = pl.BlockSpec((bm, D), lambda m: (m, 0))
    return pl.pallas_call(
        _final_norm_kernel,
        out_shape=jax.ShapeDtypeStruct((M, D), F32),
        grid=(M // bm,),
        in_specs=[rows, pl.BlockSpec((1, D), lambda m: (0, 0))],
        out_specs=rows,
        compiler_params=_compiler_params(("arbitrary",), 4 * _nbytes((bm, D), F32)),
        name="final_norm",
    )(x, g.reshape(1, D))


def _pick_block(size, preferred):
    b = min(size, preferred)
    while size % b:
        b //= 2
    return b


def kernel(x_prompt, x_sample, cache_k, cache_v, state_conv, page_table, attn_norm, w_in, conv_w,
           lambda_q1, lambda_k1, lambda_q2, lambda_k2, subln_g, w_out, ffn_norm, w_gate, w_up,
           w_down, final_norm):
    B, T, D = x_prompt.shape
    DB, S, _ = x_sample.shape
    depth, pool, page, H, E = cache_k.shape
    assert E == V7X_LANES and H == V7X_SUBLANES, "one (head, value) tile per position is assumed"
    head_dim = E // 2
    W = H * E
    C = conv_w.shape[2]
    past = page_table.shape[1] * page
    Mp, Ms = B * T, DB * S

    cos_p, sin_p = _rope_tables(np.arange(T), head_dim)
    cos_s, sin_s = _rope_tables(np.tile(past + np.arange(S), DB), head_dim)
    ck = cache_k.reshape(depth, pool, page * H, E)
    cv = cache_v.reshape(depth, pool, page * H, E)

    bm_attn = _pick_block(T, 512)
    bm_conv = _pick_block(T, 512)
    bm_out = _pick_block(T, 1024)
    bm_ff = _pick_block(T, 1024)
    bm_down = _pick_block(T, 512)
    blk = _pick_block(T, 512)
    bc = _pick_block(C, 512)
    bn = _pick_block(D, 512)
    bn_ff = _pick_block(w_gate.shape[2], 512)
    pages_per_step = _pick_block(page_table.shape[1], 8)

    xp = x_prompt.reshape(Mp, D)
    xs = x_sample.reshape(Ms, D)
    xpg, pssq = _prep(xp, attn_norm[0], bm=bm_out)
    xsg, sssq = _prep(xs, attn_norm[0], bm=Ms)
    kv_prompt = (jnp.zeros((depth, Mp * H, E), F32), jnp.zeros((depth, Mp * H, E), F32))
    cp_l, ks_l, vs_l, cs_l = [], [], [], []
    for l in range(depth):
        lam_init = 0.8 - 0.6 * math.exp(-0.3 * l)
        lamv = jnp.stack([lambda_q1[l], lambda_k1[l], lambda_q2[l], lambda_k2[l]])
        g_next = attn_norm[l + 1] if l + 1 < depth else final_norm

        qkv, k_all, v_all = _attn_proj(xpg, pssq, w_in, cos_p, sin_p, kv_prompt, layer=l, slab=l,
                                       n_slabs=depth, bm=bm_attn, n_heads=H)
        kv_prompt = (k_all, v_all)
        yc, tail = _conv_proj(xpg, pssq, w_in, conv_w, None, layer=l, bm=bm_conv, bc=bc, seq=T)
        ao = _prompt_attn(qkv, lamv, subln_g[l], batch=B, blk=blk, lam_init=lam_init)
        xp, xpg, pssq = _out_proj(ao, yc, w_out, xp, ffn_norm[l], layer=l, bm=bm_out, bn=bn)
        act = _gate_up(xpg, pssq, w_gate, w_up, layer=l, bm=bm_ff, bn=bn_ff)
        xp, xpg, pssq = _down(act, w_down, xp, g_next, layer=l, bm=bm_down, bn=bn)
        cp_l.append(tail[:, V7X_SUBLANES - (CONV_K - 1):])

        qkv, k_new, v_new = _attn_proj(xsg, sssq, w_in, cos_s, sin_s, None, layer=l, slab=0, n_slabs=1,
                                       bm=Ms, n_heads=H)
        k_new = k_new.reshape(DB, S, H, E)
        v_new = v_new.reshape(DB, S, H, E)
        yc, u = _conv_proj(xsg, sssq, w_in, conv_w, _sample_conv_prev(state_conv[l], S),
                           layer=l, bm=Ms, bc=bc, seq=S)
        ao = _sample_attn(qkv[:, :W].astype(F32).reshape(DB, S, H, E), k_new, v_new, ck, cv, page_table,
                          lamv, subln_g[l], layer=l, pages_per_step=pages_per_step, lam_init=lam_init)
        xs, xsg, sssq = _out_proj(ao.reshape(Ms, W).astype(BF16), yc, w_out, xs, ffn_norm[l],
                                  layer=l, bm=Ms, bn=bn)
        act = _gate_up(xsg, sssq, w_gate, w_up, layer=l, bm=Ms, bn=bn_ff)
        xs, xsg, sssq = _down(act, w_down, xs, g_next, layer=l, bm=Ms, bn=bn)
        ks_l.append(k_new)
        vs_l.append(v_new)
        cs_l.append(u.reshape(DB, S, C)[:, S - (CONV_K - 1):])

    y_prompt = _final_norm(xp, final_norm, bm=bm_out).reshape(B, T, D)
    y_sample = _final_norm(xs, final_norm, bm=Ms).reshape(DB, S, D)
    k_prompt, v_prompt = (a.reshape(depth, B, T, H, E) for a in kv_prompt)
    return (y_prompt, y_sample, k_prompt, v_prompt, jnp.stack(cp_l),
            jnp.stack(ks_l), jnp.stack(vs_l), jnp.stack(cs_l))
```

```python
import functools
import math

import numpy as np
import jax
import jax.numpy as jnp
from jax import lax
from jax.experimental import pallas as pl
from jax.experimental.pallas import tpu as pltpu

ROPE_THETA = 10000.0
RMS_EPS = 1e-6
SUBLN_EPS = 1e-5
CONV_K = 3
MASK_VALUE = -1e30
LOG2_E = math.log2(math.e)

V7X_LANES = 128
V7X_SUBLANES = 8
V7X_VMEM_BYTES = 64 * 1024 * 1024
MXU_WIDTH = 256
VMEM_TEMP_BYTES = 12 * 1024 * 1024

F32 = jnp.float32
BF16 = jnp.bfloat16
NT_DIMS = (((1,), (1,)), ((), ()))


def _compiler_params(semantics, block_bytes):
    limit = min(block_bytes + VMEM_TEMP_BYTES, V7X_VMEM_BYTES - 6 * 1024 * 1024)
    return pltpu.CompilerParams(dimension_semantics=semantics, vmem_limit_bytes=int(limit))


def _nbytes(shape, dtype):
    return int(np.prod(shape)) * jnp.dtype(dtype).itemsize


def _rms_scale(x, eps):
    return x * lax.rsqrt(jnp.mean(x * x, axis=-1, keepdims=True) + eps)


def _row_rsqrt(ssq_ref, width):
    return lax.rsqrt(jnp.sum(ssq_ref[...], axis=1, keepdims=True) * (1.0 / width) + RMS_EPS)


def _lane_fold(x, op):
    return functools.reduce(op, [x[:, j * V7X_LANES:(j + 1) * V7X_LANES] for j in range(x.shape[1] // V7X_LANES)])


def _emit_norm_inputs(x, g_ref, xg_ref, ssq_ref):
    xg_ref[...] = (x * g_ref[...]).astype(BF16)
    ssq_ref[...] = _lane_fold(x * x, jnp.add)


def _lambda_value(lamv_ref, lam_init):
    d1 = jnp.sum(lamv_ref[0:1, :] * lamv_ref[1:2, :], axis=1, keepdims=True)
    d2 = jnp.sum(lamv_ref[2:3, :] * lamv_ref[3:4, :], axis=1, keepdims=True)
    return jnp.exp(d1) - jnp.exp(d2) + lam_init


def _head_finish(acc, l, lam, g, out_scale, rows):
    o = acc[0:rows] / l[0:rows] - lam * (acc[rows:2 * rows] / l[rows:2 * rows])
    return (_rms_scale(o, SUBLN_EPS) * g) * out_scale


def _prep_kernel(x_ref, g_ref, xg_ref, ssq_ref):
    _emit_norm_inputs(x_ref[...], g_ref, xg_ref, ssq_ref)


def _prep(x, g, *, bm):
    M, D = x.shape
    rows = lambda width: pl.BlockSpec((bm, width), lambda m: (m, 0))
    return pl.pallas_call(
        _prep_kernel,
        out_shape=[jax.ShapeDtypeStruct((M, D), BF16), jax.ShapeDtypeStruct((M, V7X_LANES), F32)],
        grid=(M // bm,),
        in_specs=[rows(D), pl.BlockSpec((1, D), lambda m: (0, 0))],
        out_specs=[rows(D), rows(V7X_LANES)],
        compiler_params=_compiler_params(("arbitrary",), 3 * _nbytes((bm, D), F32)),
        name="prep",
    )(x, g.reshape(1, D))


def _rope_tables(positions, head_dim):
    inv = 1.0 / (ROPE_THETA ** (np.arange(0, head_dim, 2, dtype=np.float64) / head_dim))
    ang = np.asarray(positions, np.float64)[:, None] * inv[None, :]
    ang = np.concatenate([ang, ang], axis=-1)
    sign = np.concatenate([-np.ones(head_dim // 2), np.ones(head_dim // 2)])
    reps = V7X_LANES // head_dim
    cos = np.tile(np.cos(ang), (1, reps)).astype(np.float32)
    sin = np.tile(np.sin(ang) * sign[None, :], (1, reps)).astype(np.float32)
    return jnp.asarray(cos), jnp.asarray(sin)


def _rope(zc, cos, sin, low, half):
    ahead = pltpu.roll(zc, V7X_LANES - half, axis=1)
    behind = pltpu.roll(zc, half, axis=1)
    return zc * cos + jnp.where(low, ahead, behind) * sin


def _attn_proj_kernel(xg_ref, ssq_ref, w_ref, cos_ref, sin_ref, *refs, half, q_scale):
    qkv_ref, k_ref, v_ref, w_scr = refs[-4:]
    n = pl.program_id(0)
    bm, W = qkv_ref.shape
    E = k_ref.shape[1]
    H = W // E

    @pl.when(pl.program_id(1) == 0)
    def _():
        w_scr[...] = w_ref[...].astype(BF16)

    r = _row_rsqrt(ssq_ref, xg_ref.shape[1])
    lane = lax.broadcasted_iota(jnp.int32, (bm, E), 1)
    low = (lane % (2 * half)) < half

    def heads():
        per_group = MXU_WIDTH // E
        for c in range(H // per_group):
            cols = slice(c * MXU_WIDTH, (c + 1) * MXU_WIDTH)
            z = jnp.dot(xg_ref[...], w_scr[:, cols], preferred_element_type=F32) * r
            for j in range(per_group):
                yield c * per_group + j, z[:, j * E:(j + 1) * E]

    @pl.when(n == 0)
    def _():
        for h, z in heads():
            q = _rope(z, cos_ref[...], sin_ref[...], low, half)
            qkv_ref[:, h * E:(h + 1) * E] = (q * q_scale).astype(BF16)

    @pl.when(n == 1)
    def _():
        for h, z in heads():
            k = _rope(z, cos_ref[...], sin_ref[...], low, half)
            qkv_ref[:, h * E:(h + 1) * E] = k.astype(BF16)
            k_ref[pl.ds(h, bm, stride=H), :] = k

    @pl.when(n == 2)
    def _():
        for h, z in heads():
            qkv_ref[:, h * E:(h + 1) * E] = z.astype(BF16)
            v_ref[pl.ds(h, bm, stride=H), :] = z


def _attn_proj(xg, ssq, w_in, cos, sin, kv_all, *, layer, slab, n_slabs, bm, n_heads):
    M, D = xg.shape
    W = w_in.shape[2] // 6
    H = n_heads
    E = W // H
    head_dim = E // 2
    n_row_tiles = M // bm
    n_pos_blocks = cos.shape[0] // bm
    k_rows = lambda n, m: (slab, jnp.where(n == 1, m, jnp.where(n < 1, 0, n_row_tiles - 1)), 0)
    v_rows = lambda n, m: (slab, jnp.where(n == 2, m, 0), 0)
    stacked = jax.ShapeDtypeStruct((n_slabs, M * H, E), F32)
    aliases = {} if kv_all is None else {5: 1, 6: 2}
    extra_in = [] if kv_all is None else [pl.BlockSpec(memory_space=pl.ANY)] * 2
    block_bytes = 2 * (_nbytes((bm, D), BF16) + _nbytes((bm, W), BF16) + 2 * _nbytes((bm, W), F32)) \
        + _nbytes((D, W), F32) + _nbytes((D, W), BF16) + 4 * _nbytes((bm, MXU_WIDTH), F32)
    return pl.pallas_call(
        functools.partial(_attn_proj_kernel, half=head_dim // 2, q_scale=head_dim ** -0.5 * LOG2_E),
        out_shape=[jax.ShapeDtypeStruct((M, 3 * W), BF16), stacked, stacked],
        grid=(3, n_row_tiles),
        in_specs=[
            pl.BlockSpec((bm, D), lambda n, m: (m, 0)),
            pl.BlockSpec((bm, ssq.shape[1]), lambda n, m: (m, 0)),
            pl.BlockSpec((None, D, W), lambda n, m: (layer, 0, n), pipeline_mode=pl.Buffered(1)),
            pl.BlockSpec((bm, V7X_LANES), lambda n, m: (m % n_pos_blocks, 0)),
            pl.BlockSpec((bm, V7X_LANES), lambda n, m: (m % n_pos_blocks, 0)),
        ] + extra_in,
        out_specs=[
            pl.BlockSpec((bm, W), lambda n, m: (m, n)),
            pl.BlockSpec((None, bm * H, E), k_rows),
            pl.BlockSpec((None, bm * H, E), v_rows),
        ],
        scratch_shapes=[pltpu.VMEM((D, W), BF16)],
        input_output_aliases=aliases,
        compiler_params=_compiler_params(("arbitrary", "arbitrary"), block_bytes),
        name="attn_proj",
    )(xg, ssq, w_in, cos, sin, *([] if kv_all is None else kv_all))


def _conv_proj_kernel(xg_ref, ssq_ref, wb_ref, wc_ref, wh_ref, cw_ref, *refs, seq, tiles_per_seq):
    carried = tiles_per_seq is not None
    if carried:
        y_ref, tail_ref, wb_scr, wc_scr, wh_scr, carry_scr = refs
    else:
        prev1_ref, prev2_ref, y_ref, tail_ref, wb_scr, wc_scr, wh_scr = refs
    m = pl.program_id(1)
    bm, bc = y_ref.shape

    @pl.when(m == 0)
    def _():
        wb_scr[...] = wb_ref[...].astype(BF16)
        wc_scr[...] = wc_ref[...].astype(BF16)
        wh_scr[...] = wh_ref[...].astype(BF16)

    r = _row_rsqrt(ssq_ref, xg_ref.shape[1])
    if carried:
        @pl.when(m % tiles_per_seq == 0)
        def _():
            carry_scr[...] = jnp.zeros(carry_scr.shape, F32)

    bw = min(bc, MXU_WIDTH)
    t = lax.broadcasted_iota(jnp.int32, (bm, bw), 0)
    for c in range(bc // bw):
        cols = slice(c * bw, (c + 1) * bw)
        gb = jnp.dot(xg_ref[...], wb_scr[:, cols], preferred_element_type=F32) * r
        gc = jnp.dot(xg_ref[...], wc_scr[:, cols], preferred_element_type=F32) * r
        hc = jnp.dot(xg_ref[...], wh_scr[:, cols], preferred_element_type=F32) * r
        u = gc * hc
        if carried:
            last = jnp.broadcast_to(carry_scr[V7X_SUBLANES - 1:V7X_SUBLANES, cols], (bm, bw))
            second_last = jnp.broadcast_to(carry_scr[V7X_SUBLANES - 2:V7X_SUBLANES - 1, cols], (bm, bw))
            prev1 = last
            prev2 = jnp.where(t == 0, second_last, last)
            pos = t
        else:
            prev1 = prev1_ref[:, cols]
            prev2 = prev2_ref[:, cols]
            pos = t % seq
        sh1 = jnp.where(pos >= 1, pltpu.roll(u, 1, axis=0), prev1)
        sh2 = jnp.where(pos >= 2, pltpu.roll(u, 2, axis=0), prev2)
        y = gb * (cw_ref[0:1, cols] * sh2 + cw_ref[1:2, cols] * sh1 + cw_ref[2:3, cols] * u)
        y_ref[:, cols] = y.astype(BF16)
        if carried:
            tail = u[bm - V7X_SUBLANES:bm, :]
            carry_scr[:, cols] = tail
            tail_ref[:, cols] = tail
        else:
            tail_ref[:, cols] = u


def _conv_proj(xg, ssq, w_in, conv_w, prev, *, layer, bm, bc, seq):
    M, D = xg.shape
    C = conv_w.shape[2]
    col0 = (w_in.shape[2] - 3 * C) // bc
    w_spec = lambda j: pl.BlockSpec((None, D, bc), lambda c, m: (layer, 0, col0 + j * (C // bc) + c),
                                    pipeline_mode=pl.Buffered(1))
    tile = pl.BlockSpec((bm, bc), lambda c, m: (m, c))
    in_specs = [
        pl.BlockSpec((bm, D), lambda c, m: (m, 0)),
        pl.BlockSpec((bm, ssq.shape[1]), lambda c, m: (m, 0)),
        w_spec(0), w_spec(1), w_spec(2),
        pl.BlockSpec((None, CONV_K, bc), lambda c, m: (layer, 0, c)),
    ]
    scratch = [pltpu.VMEM((D, bc), BF16)] * 3
    if prev is None:
        tiles_per_seq = seq // bm
        batch = M // seq
        tail_shape = jax.ShapeDtypeStruct((batch, V7X_SUBLANES, C), F32)
        tail_spec = pl.BlockSpec((None, V7X_SUBLANES, bc), lambda c, m: (m // tiles_per_seq, 0, c))
        scratch = scratch + [pltpu.VMEM((V7X_SUBLANES, bc), F32)]
        operands = ()
    else:
        tiles_per_seq = None
        tail_shape = jax.ShapeDtypeStruct((M, C), F32)
        tail_spec = tile
        in_specs = in_specs + [tile, tile]
        operands = prev
    block_bytes = 2 * (_nbytes((bm, D), BF16) + 4 * _nbytes((bm, bc), F32)) \
        + 3 * _nbytes((D, bc), F32) + 3 * _nbytes((D, bc), BF16) + 8 * _nbytes((bm, MXU_WIDTH), F32)
    return pl.pallas_call(
        functools.partial(_conv_proj_kernel, seq=seq, tiles_per_seq=tiles_per_seq),
        out_shape=[jax.ShapeDtypeStruct((M, C), BF16), tail_shape],
        grid=(C // bc, M // bm),
        in_specs=in_specs,
        out_specs=[tile, tail_spec],
        scratch_shapes=scratch,
        compiler_params=_compiler_params(("arbitrary", "arbitrary"), block_bytes),
        name="conv_proj",
    )(xg, ssq, w_in, w_in, w_in, conv_w, *operands)


def _sample_conv_prev(state, seq):
    DB, _, C = state.shape
    zeros = jnp.zeros((DB, seq - 1, C), F32)
    prev1 = jnp.concatenate([state[:, 1:2], zeros], axis=1).reshape(DB * seq, C)
    prev2 = jnp.concatenate([state, zeros[:, 1:]], axis=1).reshape(DB * seq, C)
    return prev1, prev2


def _prompt_attn_kernel(lamv_ref, g_ref, q_ref, k_ref, v_ref, o_ref, qs_scr, p_scr, *, blk, lam_init):
    T, E = q_ref.shape
    half = E // 2
    lam = _lambda_value(lamv_ref, lam_init)
    lane = lax.broadcasted_iota(jnp.int32, (blk, E), 1)
    row = lax.broadcasted_iota(jnp.int32, (2 * blk, blk), 0)
    col = lax.broadcasted_iota(jnp.int32, (2 * blk, blk), 1)
    causal = col <= jnp.where(row >= blk, row - blk, row)

    def scores(c, diagonal):
        s = lax.dot_general(qs_scr[...], k_ref[c * blk:(c + 1) * blk, :], NT_DIMS, preferred_element_type=F32)
        return jnp.where(causal, s, MASK_VALUE) if diagonal else s

    for qi in range(T // blk):
        q = q_ref[qi * blk:(qi + 1) * blk, :]
        qs_scr[0:blk, :] = jnp.where(lane < half, q, jnp.zeros_like(q))
        qs_scr[blk:2 * blk, :] = jnp.where(lane >= half, q, jnp.zeros_like(q))
        m_lanes = jnp.full((2 * blk, V7X_LANES), MASK_VALUE, F32)
        for c in range(qi + 1):
            m_lanes = jnp.maximum(m_lanes, _lane_fold(scores(c, c == qi), jnp.maximum))
        m = jnp.broadcast_to(jnp.max(m_lanes, axis=1, keepdims=True), (2 * blk, V7X_LANES))
        l_lanes = jnp.zeros((2 * blk, V7X_LANES), F32)
        for c in range(qi + 1):
            s = scores(c, c == qi)
            for j in range(blk // V7X_LANES):
                p = jnp.exp2(s[:, j * V7X_LANES:(j + 1) * V7X_LANES] - m)
                l_lanes = l_lanes + p
                p_scr[:, c * blk + j * V7X_LANES:c * blk + (j + 1) * V7X_LANES] = p.astype(BF16)
        l = jnp.sum(l_lanes, axis=1, keepdims=True)
        visible = (qi + 1) * blk
        acc = jnp.dot(p_scr[:, 0:visible], v_ref[0:visible, :], preferred_element_type=F32)
        o = _head_finish(acc, l, lam, g_ref[...], 1.0 - lam_init, blk)
        o_ref[qi * blk:(qi + 1) * blk, :] = o.astype(o_ref.dtype)


def _prompt_attn(qkv, lamv, g, *, batch, blk, lam_init):
    M = qkv.shape[0]
    E = g.shape[0]
    H = qkv.shape[1] // (3 * E)
    T = M // batch
    block_bytes = 2 * 4 * _nbytes((T, E), BF16) + _nbytes((2 * blk, E), BF16) + _nbytes((2 * blk, T), BF16) \
        + 3 * _nbytes((2 * blk, blk), F32)
    return pl.pallas_call(
        functools.partial(_prompt_attn_kernel, blk=blk, lam_init=lam_init),
        out_shape=jax.ShapeDtypeStruct((M, H * E), BF16),
        grid=(batch, H),
        in_specs=[
            pl.BlockSpec(lamv.shape, lambda b, h: (0, 0)),
            pl.BlockSpec((1, E), lambda b, h: (0, 0)),
            pl.BlockSpec((T, E), lambda b, h: (b, h)),
            pl.BlockSpec((T, E), lambda b, h: (b, H + h)),
            pl.BlockSpec((T, E), lambda b, h: (b, 2 * H + h)),
        ],
        out_specs=pl.BlockSpec((T, E), lambda b, h: (b, h)),
        scratch_shapes=[pltpu.VMEM((2 * blk, E), BF16), pltpu.VMEM((2 * blk, T), BF16)],
        compiler_params=_compiler_params(("arbitrary", "arbitrary"), block_bytes),
        name="prompt_attn",
    )(lamv, g.reshape(1, E), qkv, qkv, qkv)


def _sample_attn_kernel(pt_ref, lamv_ref, g_ref, q_ref, kn_ref, vn_ref, *refs,
                        pages_per_step, lam_init):
    P = pages_per_step
    k_refs, v_refs = refs[:P], refs[P:2 * P]
    o_ref, qall_scr, bias_scr, m_scr, l_scr, acc_scr = refs[2 * P:]
    S, H, E = q_ref.shape
    half = E // 2
    i = pl.program_id(1)

    @pl.when(i == 0)
    def _():
        rr = lax.broadcasted_iota(jnp.int32, bias_scr.shape, 0)
        cc = lax.broadcasted_iota(jnp.int32, bias_scr.shape, 1)
        bias_scr[...] = jnp.where(cc % H == rr % H, 0.0, MASK_VALUE)
        lane = lax.broadcasted_iota(jnp.int32, (H, E), 1)
        for m in range(2):
            for s in range(S):
                rows = pl.ds((m * S + s) * H, H)
                q = jnp.where(lane // half == m, q_ref[s], 0.0)
                qall_scr[rows, :] = q.astype(BF16)
                sc = [jnp.sum(q * kn_ref[j], axis=1, keepdims=True) for j in range(s + 1)]
                m0 = functools.reduce(jnp.maximum, sc)
                ps = [jnp.exp2(c - m0) for c in sc]
                m_scr[rows, :] = m0
                l_scr[rows, :] = functools.reduce(jnp.add, ps)
                acc_scr[rows, :] = functools.reduce(jnp.add, [p * vn_ref[j] for j, p in enumerate(ps)])

    qall = qall_scr[...]
    bias = bias_scr[...]
    scores = []
    for j in range(P):
        kj = k_refs[j][...].astype(BF16)
        scores.append(lax.dot_general(qall, kj, NT_DIMS, preferred_element_type=F32) + bias)
    m_old = m_scr[...]
    m_new = m_old
    for s in scores:
        m_new = jnp.maximum(m_new, jnp.max(s, axis=1, keepdims=True))
    alpha = jnp.exp2(m_old - m_new)
    l_new = alpha * l_scr[...]
    acc = alpha * acc_scr[...]
    for j in range(P):
        p = jnp.exp2(scores[j] - m_new)
        l_new = l_new + jnp.sum(p, axis=1, keepdims=True)
        acc = acc + jnp.dot(p.astype(BF16), v_refs[j][...].astype(BF16), preferred_element_type=F32)
    m_scr[...] = m_new
    l_scr[...] = l_new
    acc_scr[...] = acc

    @pl.when(i == pl.num_programs(1) - 1)
    def _():
        lam = _lambda_value(lamv_ref, lam_init)
        o = _head_finish(acc_scr[...], l_scr[...], lam, g_ref[...], 1.0 - lam_init, S * H)
        o_ref[...] = o.reshape(S, H, E)


def _sample_attn(q, k_new, v_new, cache_k, cache_v, page_table, lamv, g, *, layer, pages_per_step, lam_init):
    DB, S, H, E = q.shape
    page_rows = cache_k.shape[2]
    n_pages = page_table.shape[1]
    P = pages_per_step
    R = 2 * S * H
    per_seq = pl.BlockSpec((None, S, H, E), lambda b, i, pt: (b, 0, 0, 0))

    def page_spec(j):
        return pl.BlockSpec((None, None, page_rows, E), lambda b, i, pt: (layer, pt[b, i * P + j], 0, 0))

    block_bytes = 2 * (2 * P * _nbytes((page_rows, E), F32)) + 2 * P * _nbytes((page_rows, E), BF16) \
        + (2 + P) * _nbytes((R, page_rows), F32)
    return pl.pallas_call(
        functools.partial(_sample_attn_kernel, pages_per_step=P, lam_init=lam_init),
        out_shape=jax.ShapeDtypeStruct((DB, S, H, E), F32),
        grid_spec=pltpu.PrefetchScalarGridSpec(
            num_scalar_prefetch=1,
            grid=(DB, n_pages // P),
            in_specs=[
                pl.BlockSpec(lamv.shape, lambda b, i, pt: (0, 0)),
                pl.BlockSpec((1, E), lambda b, i, pt: (0, 0)),
                per_seq, per_seq, per_seq,
            ] + [page_spec(j) for j in range(P)] + [page_spec(j) for j in range(P)],
            out_specs=per_seq,
            scratch_shapes=[
                pltpu.VMEM((R, E), BF16), pltpu.VMEM((R, page_rows), F32),
                pltpu.VMEM((R, 1), F32), pltpu.VMEM((R, 1), F32), pltpu.VMEM((R, E), F32),
            ],
        ),
        compiler_params=_compiler_params(("arbitrary", "arbitrary"), block_bytes),
        name="sample_attn",
    )(page_table, lamv, g.reshape(1, E), q, k_new, v_new, *([cache_k] * P), *([cache_v] * P))


def _residual_tile(lhs_refs, w_refs, r_ref, g_ref, x_ref, xg_ref, ssq_ref):
    bn = x_ref.shape[1]
    bw = min(bn, MXU_WIDTH)
    ssq = None
    for c in range(bn // bw):
        cols = slice(c * bw, (c + 1) * bw)
        x = r_ref[:, cols]
        for lhs_ref, w_ref in zip(lhs_refs, w_refs):
            x = x + jnp.dot(lhs_ref[...], w_ref[:, cols], preferred_element_type=F32)
        x_ref[:, cols] = x
        xg_ref[:, cols] = (x * g_ref[:, cols]).astype(BF16)
        part = _lane_fold(x * x, jnp.add)
        ssq = part if ssq is None else ssq + part
    ssq_ref[...] = ssq


def _out_proj_kernel(a_ref, y_ref, w_ref, r_ref, g_ref, x_ref, xg_ref, ssq_ref, w_scr):
    @pl.when(pl.program_id(0) == 0)
    def _():
        w_scr[...] = w_ref[...].astype(BF16)

    ka = a_ref.shape[1]
    _residual_tile((a_ref, y_ref), (w_scr.at[0:ka, :], w_scr.at[ka:, :]), r_ref, g_ref, x_ref, xg_ref, ssq_ref)


def _out_proj(a, y, w, res, g, *, layer, bm):
    M, ka = a.shape
    K = ka + y.shape[1]
    N = w.shape[2]
    rows = lambda width: pl.BlockSpec((bm, width), lambda m: (m, 0))
    block_bytes = 2 * (_nbytes((bm, K), BF16) + 2 * _nbytes((bm, N), F32) + _nbytes((bm, N), BF16)) \
        + _nbytes((K, N), F32) + _nbytes((K, N), BF16) + 4 * _nbytes((bm, MXU_WIDTH), F32)
    return pl.pallas_call(
        _out_proj_kernel,
        out_shape=[jax.ShapeDtypeStruct((M, N), F32), jax.ShapeDtypeStruct((M, N), BF16),
                   jax.ShapeDtypeStruct((M, V7X_LANES), F32)],
        grid=(M // bm,),
        in_specs=[
            rows(ka), rows(K - ka),
            pl.BlockSpec((None, K, N), lambda m: (layer, 0, 0), pipeline_mode=pl.Buffered(1)),
            rows(N),
            pl.BlockSpec((1, N), lambda m: (0, 0)),
        ],
        out_specs=[rows(N), rows(N), rows(V7X_LANES)],
        scratch_shapes=[pltpu.VMEM((K, N), BF16)],
        compiler_params=_compiler_params(("arbitrary",), block_bytes),
        name="out_proj",
    )(a, y, w, res, g.reshape(1, N))


def _swiglu_tile(xg_ref, ssq_ref, wg_scr, wu_scr, o_ref):
    r = _row_rsqrt(ssq_ref, xg_ref.shape[1])
    bn = o_ref.shape[1]
    bw = min(bn, MXU_WIDTH)
    for c in range(bn // bw):
        cols = slice(c * bw, (c + 1) * bw)
        gate = jnp.dot(xg_ref[...], wg_scr[:, cols], preferred_element_type=F32) * r
        up = jnp.dot(xg_ref[...], wu_scr[:, cols], preferred_element_type=F32) * r
        o_ref[:, cols] = ((gate / (1.0 + jnp.exp(-gate))) * up).astype(BF16)


def _gate_up_kernel(xg_ref, ssq_ref, xsg_ref, sssq_ref, wg_ref, wu_ref, wd_ref,
                    o_ref, os_ref, wdb_ref, wg_scr, wu_scr):
    @pl.when(pl.program_id(1) == 0)
    def _():
        wg_scr[...] = wg_ref[...].astype(BF16)
        wu_scr[...] = wu_ref[...].astype(BF16)
        _swiglu_tile(xsg_ref, sssq_ref, wg_scr, wu_scr, os_ref)

    wdb_ref[...] = wd_ref[...].astype(BF16)
    _swiglu_tile(xg_ref, ssq_ref, wg_scr, wu_scr, o_ref)


def _gate_up(xg, ssq, xsg, sssq, wg, wu, wd, *, layer, bm, bn):
    M, D = xg.shape
    Ms = xsg.shape[0]
    F, N = wd.shape[1:]
    n_m = M // bm
    slab = F // (F // bn * n_m)
    w_spec = pl.BlockSpec((None, D, bn), lambda n, m: (layer, 0, n))
    block_bytes = 2 * (_nbytes((bm, D), BF16) + 2 * _nbytes((D, bn), F32) + _nbytes((bm, bn), BF16)
                       + _nbytes((slab, N), F32) + _nbytes((slab, N), BF16) + _nbytes((Ms, D), BF16)) \
        + 2 * _nbytes((D, bn), BF16) + 6 * _nbytes((bm, MXU_WIDTH), F32)
    return pl.pallas_call(
        _gate_up_kernel,
        out_shape=[jax.ShapeDtypeStruct((M, F), BF16), jax.ShapeDtypeStruct((Ms, F), BF16),
                   jax.ShapeDtypeStruct((F, N), BF16)],
        grid=(F // bn, n_m),
        in_specs=[
            pl.BlockSpec((bm, D), lambda n, m: (m, 0)),
            pl.BlockSpec((bm, ssq.shape[1]), lambda n, m: (m, 0)),
            pl.BlockSpec((Ms, D), lambda n, m: (0, 0)),
            pl.BlockSpec((Ms, sssq.shape[1]), lambda n, m: (0, 0)),
            w_spec, w_spec,
            pl.BlockSpec((None, slab, N), lambda n, m: (layer, n * n_m + m, 0)),
        ],
        out_specs=[
            pl.BlockSpec((bm, bn), lambda n, m: (m, n)),
            pl.BlockSpec((Ms, bn), lambda n, m: (0, n)),
            pl.BlockSpec((slab, N), lambda n, m: (n * n_m + m, 0)),
        ],
        scratch_shapes=[pltpu.VMEM((D, bn), BF16)] * 2,
        compiler_params=_compiler_params(("arbitrary", "arbitrary"), block_bytes),
        name="gate_up",
    )(xg, ssq, xsg, sssq, wg, wu, wd)


def _down_kernel(a_ref, w_ref, r_ref, g_ref, x_ref, xg_ref, ssq_ref):
    _residual_tile((a_ref,), (w_ref,), r_ref, g_ref, x_ref, xg_ref, ssq_ref)


def _down(a, w, res, g, *, bm, bn):
    M, F = a.shape
    N = w.shape[1]
    tile = lambda width: pl.BlockSpec((bm, width), lambda m, n: (m, n))
    block_bytes = 2 * (_nbytes((bm, F), BF16) + _nbytes((F, bn), BF16) + 2 * _nbytes((bm, bn), F32)
                       + _nbytes((bm, bn), BF16)) + 4 * _nbytes((bm, MXU_WIDTH), F32)
    return pl.pallas_call(
        _down_kernel,
        out_shape=[jax.ShapeDtypeStruct((M, N), F32), jax.ShapeDtypeStruct((M, N), BF16),
                   jax.ShapeDtypeStruct((M, N // bn * V7X_LANES), F32)],
        grid=(M // bm, N // bn),
        in_specs=[
            pl.BlockSpec((bm, F), lambda m, n: (m, 0)),
            pl.BlockSpec((F, bn), lambda m, n: (0, n)),
            tile(bn),
            pl.BlockSpec((1, bn), lambda m, n: (0, n)),
        ],
        out_specs=[tile(bn), tile(bn), tile(V7X_LANES)],
        compiler_params=_compiler_params(("arbitrary", "arbitrary"), block_bytes),
        name="down_proj",
    )(a, w, res, g.reshape(1, N))


def _final_norm_kernel(x_ref, g_ref, o_ref):
    o_ref[...] = _rms_scale(x_ref[...], RMS_EPS) * g_ref[...]


def _final_norm(x, g, *, bm):
    M, D = x.shape
    rows = pl.BlockSpec((bm, D), lambda m: (m, 0))
    return pl.pallas_call(
        _final_norm_kernel,
        out_shape=jax.ShapeDtypeStruct((M, D), F32),
        grid=(M // bm,),
        in_specs=[rows, pl.BlockSpec((1, D), lambda m: (0, 0))],
        out_specs=rows,
        compiler_params=_compiler_params(("arbitrary",), 4 * _nbytes((bm, D), F32)),
        name="final_norm",
    )(x, g.reshape(1, D))


def _pick_block(size, preferred):
    b = min(size, preferred)
    while size % b:
        b //= 2
    return b


def kernel(x_prompt, x_sample, cache_k, cache_v, state_conv, page_table, attn_norm, w_in, conv_w,
           lambda_q1, lambda_k1, lambda_q2, lambda_k2, subln_g, w_out, ffn_norm, w_gate, w_up,
           w_down, final_norm):
    B, T, D = x_prompt.shape
    DB, S, _ = x_sample.shape
    depth, pool, page, H, E = cache_k.shape
    assert E == V7X_LANES and H == V7X_SUBLANES, "one (head, value) tile per position is assumed"
    head_dim = E // 2
    W = H * E
    C = conv_w.shape[2]
    past = page_table.shape[1] * page
    Mp, Ms = B * T, DB * S

    cos_p, sin_p = _rope_tables(np.arange(T), head_dim)
    cos_s, sin_s = _rope_tables(np.tile(past + np.arange(S), DB), head_dim)
    ck = cache_k.reshape(depth, pool, page * H, E)
    cv = cache_v.reshape(depth, pool, page * H, E)

    bm_attn = _pick_block(T, 1024)
    bm_conv = _pick_block(T, 1024)
    bm_out = _pick_block(T, 512)
    bm_ff = _pick_block(T, 1024)
    bm_down = _pick_block(T, 1024)
    blk = _pick_block(T, 512)
    bc = _pick_block(C, 512)
    bn = _pick_block(D, 512)
    bn_ff = _pick_block(w_gate.shape[2], 512)
    pages_per_step = _pick_block(page_table.shape[1], 8)

    xp = x_prompt.reshape(Mp, D)
    xs = x_sample.reshape(Ms, D)
    xpg, pssq = _prep(xp, attn_norm[0], bm=bm_ff)
    xsg, sssq = _prep(xs, attn_norm[0], bm=Ms)
    kv_prompt = (jnp.zeros((depth, Mp * H, E), F32), jnp.zeros((depth, Mp * H, E), F32))
    cp_l, ks_l, vs_l, cs_l = [], [], [], []
    for l in range(depth):
        lam_init = 0.8 - 0.6 * math.exp(-0.3 * l)
        lamv = jnp.stack([lambda_q1[l], lambda_k1[l], lambda_q2[l], lambda_k2[l]])
        g_next = attn_norm[l + 1] if l + 1 < depth else final_norm

        qkv, k_all, v_all = _attn_proj(xpg, pssq, w_in, cos_p, sin_p, kv_prompt, layer=l, slab=l,
                                       n_slabs=depth, bm=bm_attn, n_heads=H)
        kv_prompt = (k_all, v_all)
        yc, tail = _conv_proj(xpg, pssq, w_in, conv_w, None, layer=l, bm=bm_conv, bc=bc, seq=T)
        ao = _prompt_attn(qkv, lamv, subln_g[l], batch=B, blk=blk, lam_init=lam_init)
        xp, xpg, pssq = _out_proj(ao, yc, w_out, xp, ffn_norm[l], layer=l, bm=bm_out)
        cp_l.append(tail[:, V7X_SUBLANES - (CONV_K - 1):])

        qkv, k_new, v_new = _attn_proj(xsg, sssq, w_in, cos_s, sin_s, None, layer=l, slab=0, n_slabs=1,
                                       bm=Ms, n_heads=H)
        k_new = k_new.reshape(DB, S, H, E)
        v_new = v_new.reshape(DB, S, H, E)
        yc, u = _conv_proj(xsg, sssq, w_in, conv_w, _sample_conv_prev(state_conv[l], S),
                           layer=l, bm=Ms, bc=bc, seq=S)
        ao = _sample_attn(qkv[:, :W].astype(F32).reshape(DB, S, H, E), k_new, v_new, ck, cv, page_table,
                          lamv, subln_g[l], layer=l, pages_per_step=pages_per_step, lam_init=lam_init)
        xs, xsg, sssq = _out_proj(ao.reshape(Ms, W).astype(BF16), yc, w_out, xs, ffn_norm[l], layer=l, bm=Ms)

        act_p, act_s, wd = _gate_up(xpg, pssq, xsg, sssq, w_gate, w_up, w_down, layer=l, bm=bm_ff, bn=bn_ff)
        xp, xpg, pssq = _down(act_p, wd, xp, g_next, bm=bm_down, bn=bn)
        xs, xsg, sssq = _down(act_s, wd, xs, g_next, bm=Ms, bn=bn)
        ks_l.append(k_new)
        vs_l.append(v_new)
        cs_l.append(u.reshape(DB, S, C)[:, S - (CONV_K - 1):])

    y_prompt = _final_norm(xp, final_norm, bm=bm_out).reshape(B, T, D)
    y_sample = _final_norm(xs, final_norm, bm=Ms).reshape(DB, S, D)
    k_prompt, v_prompt = (a.reshape(depth, B, T, H, E) for a in kv_prompt)
    return (y_prompt, y_sample, k_prompt, v_prompt, jnp.stack(cp_l),
            jnp.stack(ks_l), jnp.stack(vs_l), jnp.stack(cs_l))
```

```python
import functools
import math

import numpy as np
import jax
import jax.numpy as jnp
from jax import lax
from jax.experimental import pallas as pl
from jax.experimental.pallas import tpu as pltpu

ROPE_THETA = 10000.0
RMS_EPS = 1e-6
SUBLN_EPS = 1e-5
CONV_K = 3
MASK_VALUE = -1e30
LOG2_E = math.log2(math.e)

V7X_LANES = 128
V7X_SUBLANES = 8
V7X_VMEM_BYTES = 64 * 1024 * 1024
MXU_WIDTH = 256
VMEM_TEMP_BYTES = 12 * 1024 * 1024

F32 = jnp.float32
BF16 = jnp.bfloat16
NT_DIMS = (((1,), (1,)), ((), ()))


def _compiler_params(semantics, block_bytes):
    limit = min(block_bytes + VMEM_TEMP_BYTES, V7X_VMEM_BYTES - 6 * 1024 * 1024)
    return pltpu.CompilerParams(dimension_semantics=semantics, vmem_limit_bytes=int(limit))


def _weight_buffering(n_row_tiles):
    return pl.Buffered(1) if n_row_tiles > 1 else None


def _nbytes(shape, dtype):
    return int(np.prod(shape)) * jnp.dtype(dtype).itemsize


def _rms_scale(x, eps):
    return x * lax.rsqrt(jnp.mean(x * x, axis=-1, keepdims=True) + eps)


def _row_rsqrt(ssq_ref, width):
    return lax.rsqrt(jnp.sum(ssq_ref[...], axis=1, keepdims=True) * (1.0 / width) + RMS_EPS)


def _lane_fold(x, op):
    return functools.reduce(op, [x[:, j * V7X_LANES:(j + 1) * V7X_LANES] for j in range(x.shape[1] // V7X_LANES)])


def _emit_norm_inputs(x, g_ref, xg_ref, ssq_ref):
    xg_ref[...] = (x * g_ref[...]).astype(BF16)
    ssq_ref[...] = _lane_fold(x * x, jnp.add)


def _lambda_value(lamv_ref, lam_init):
    d1 = jnp.sum(lamv_ref[0:1, :] * lamv_ref[1:2, :], axis=1, keepdims=True)
    d2 = jnp.sum(lamv_ref[2:3, :] * lamv_ref[3:4, :], axis=1, keepdims=True)
    return jnp.exp(d1) - jnp.exp(d2) + lam_init


def _head_finish(acc, l, lam, g, out_scale, rows):
    o = acc[0:rows] / l[0:rows] - lam * (acc[rows:2 * rows] / l[rows:2 * rows])
    return (_rms_scale(o, SUBLN_EPS) * g) * out_scale


def _prep_kernel(x_ref, g_ref, xg_ref, ssq_ref):
    _emit_norm_inputs(x_ref[...], g_ref, xg_ref, ssq_ref)


def _prep(x, g, *, bm):
    M, D = x.shape
    rows = lambda width: pl.BlockSpec((bm, width), lambda m: (m, 0))
    return pl.pallas_call(
        _prep_kernel,
        out_shape=[jax.ShapeDtypeStruct((M, D), BF16), jax.ShapeDtypeStruct((M, V7X_LANES), F32)],
        grid=(M // bm,),
        in_specs=[rows(D), pl.BlockSpec((1, D), lambda m: (0, 0))],
        out_specs=[rows(D), rows(V7X_LANES)],
        compiler_params=_compiler_params(("arbitrary",), 3 * _nbytes((bm, D), F32)),
        name="prep",
    )(x, g.reshape(1, D))


def _rope_tables(positions, head_dim):
    inv = 1.0 / (ROPE_THETA ** (np.arange(0, head_dim, 2, dtype=np.float64) / head_dim))
    ang = np.asarray(positions, np.float64)[:, None] * inv[None, :]
    ang = np.concatenate([ang, ang], axis=-1)
    sign = np.concatenate([-np.ones(head_dim // 2), np.ones(head_dim // 2)])
    reps = V7X_LANES // head_dim
    cos = np.tile(np.cos(ang), (1, reps)).astype(np.float32)
    sin = np.tile(np.sin(ang) * sign[None, :], (1, reps)).astype(np.float32)
    return jnp.asarray(cos), jnp.asarray(sin)


def _rope(zc, cos, sin, low, half):
    ahead = pltpu.roll(zc, V7X_LANES - half, axis=1)
    behind = pltpu.roll(zc, half, axis=1)
    return zc * cos + jnp.where(low, ahead, behind) * sin


def _attn_proj_kernel(xg_ref, ssq_ref, w_ref, cos_ref, sin_ref, *refs, half, q_scale):
    qkv_ref, k_ref, v_ref, w_scr = refs[-4:]
    n = pl.program_id(0)
    bm, W = qkv_ref.shape
    E = k_ref.shape[1]
    H = W // E

    @pl.when(pl.program_id(1) == 0)
    def _():
        w_scr[...] = w_ref[...].astype(BF16)

    r = _row_rsqrt(ssq_ref, xg_ref.shape[1])
    lane = lax.broadcasted_iota(jnp.int32, (bm, E), 1)
    low = (lane % (2 * half)) < half

    def heads():
        per_group = MXU_WIDTH // E
        for c in range(H // per_group):
            cols = slice(c * MXU_WIDTH, (c + 1) * MXU_WIDTH)
            z = jnp.dot(xg_ref[...], w_scr[:, cols], preferred_element_type=F32) * r
            for j in range(per_group):
                yield c * per_group + j, z[:, j * E:(j + 1) * E]

    @pl.when(n == 0)
    def _():
        for h, z in heads():
            q = _rope(z, cos_ref[...], sin_ref[...], low, half)
            qkv_ref[:, h * E:(h + 1) * E] = (q * q_scale).astype(BF16)

    @pl.when(n == 1)
    def _():
        for h, z in heads():
            k = _rope(z, cos_ref[...], sin_ref[...], low, half)
            qkv_ref[:, h * E:(h + 1) * E] = k.astype(BF16)
            k_ref[pl.ds(h, bm, stride=H), :] = k

    @pl.when(n == 2)
    def _():
        for h, z in heads():
            qkv_ref[:, h * E:(h + 1) * E] = z.astype(BF16)
            v_ref[pl.ds(h, bm, stride=H), :] = z


def _attn_proj(xg, ssq, w_in, cos, sin, kv_all, *, layer, slab, n_slabs, bm, n_heads):
    M, D = xg.shape
    W = w_in.shape[2] // 6
    H = n_heads
    E = W // H
    head_dim = E // 2
    n_row_tiles = M // bm
    n_pos_blocks = cos.shape[0] // bm
    k_rows = lambda n, m: (slab, jnp.where(n == 1, m, jnp.where(n < 1, 0, n_row_tiles - 1)), 0)
    v_rows = lambda n, m: (slab, jnp.where(n == 2, m, 0), 0)
    stacked = jax.ShapeDtypeStruct((n_slabs, M * H, E), F32)
    aliases = {} if kv_all is None else {5: 1, 6: 2}
    extra_in = [] if kv_all is None else [pl.BlockSpec(memory_space=pl.ANY)] * 2
    block_bytes = 2 * (_nbytes((bm, D), BF16) + _nbytes((bm, W), BF16) + 2 * _nbytes((bm, W), F32)) \
        + _nbytes((D, W), F32) + _nbytes((D, W), BF16) + 4 * _nbytes((bm, MXU_WIDTH), F32)
    return pl.pallas_call(
        functools.partial(_attn_proj_kernel, half=head_dim // 2, q_scale=head_dim ** -0.5 * LOG2_E),
        out_shape=[jax.ShapeDtypeStruct((M, 3 * W), BF16), stacked, stacked],
        grid=(3, n_row_tiles),
        in_specs=[
            pl.BlockSpec((bm, D), lambda n, m: (m, 0)),
            pl.BlockSpec((bm, ssq.shape[1]), lambda n, m: (m, 0)),
            pl.BlockSpec((None, D, W), lambda n, m: (layer, 0, n), pipeline_mode=_weight_buffering(n_row_tiles)),
            pl.BlockSpec((bm, V7X_LANES), lambda n, m: (m % n_pos_blocks, 0)),
            pl.BlockSpec((bm, V7X_LANES), lambda n, m: (m % n_pos_blocks, 0)),
        ] + extra_in,
        out_specs=[
            pl.BlockSpec((bm, W), lambda n, m: (m, n)),
            pl.BlockSpec((None, bm * H, E), k_rows),
            pl.BlockSpec((None, bm * H, E), v_rows),
        ],
        scratch_shapes=[pltpu.VMEM((D, W), BF16)],
        input_output_aliases=aliases,
        compiler_params=_compiler_params(("arbitrary", "arbitrary"), block_bytes),
        name="attn_proj",
    )(xg, ssq, w_in, cos, sin, *([] if kv_all is None else kv_all))


def _conv_proj_kernel(xg_ref, ssq_ref, wb_ref, wc_ref, wh_ref, cw_ref, *refs, seq, tiles_per_seq):
    carried = tiles_per_seq is not None
    if carried:
        y_ref, tail_ref, wb_scr, wc_scr, wh_scr, carry_scr = refs
    else:
        prev1_ref, prev2_ref, y_ref, tail_ref, wb_scr, wc_scr, wh_scr = refs
    m = pl.program_id(1)
    bm, bc = y_ref.shape

    @pl.when(m == 0)
    def _():
        wb_scr[...] = wb_ref[...].astype(BF16)
        wc_scr[...] = wc_ref[...].astype(BF16)
        wh_scr[...] = wh_ref[...].astype(BF16)

    r = _row_rsqrt(ssq_ref, xg_ref.shape[1])
    if carried:
        @pl.when(m % tiles_per_seq == 0)
        def _():
            carry_scr[...] = jnp.zeros(carry_scr.shape, F32)

    bw = min(bc, MXU_WIDTH)
    t = lax.broadcasted_iota(jnp.int32, (bm, bw), 0)
    for c in range(bc // bw):
        cols = slice(c * bw, (c + 1) * bw)
        gb = jnp.dot(xg_ref[...], wb_scr[:, cols], preferred_element_type=F32) * r
        gc = jnp.dot(xg_ref[...], wc_scr[:, cols], preferred_element_type=F32) * r
        hc = jnp.dot(xg_ref[...], wh_scr[:, cols], preferred_element_type=F32) * r
        u = gc * hc
        if carried:
            last = jnp.broadcast_to(carry_scr[V7X_SUBLANES - 1:V7X_SUBLANES, cols], (bm, bw))
            second_last = jnp.broadcast_to(carry_scr[V7X_SUBLANES - 2:V7X_SUBLANES - 1, cols], (bm, bw))
            prev1 = last
            prev2 = jnp.where(t == 0, second_last, last)
            pos = t
        else:
            prev1 = prev1_ref[:, cols]
            prev2 = prev2_ref[:, cols]
            pos = t % seq
        sh1 = jnp.where(pos >= 1, pltpu.roll(u, 1, axis=0), prev1)
        sh2 = jnp.where(pos >= 2, pltpu.roll(u, 2, axis=0), prev2)
        y = gb * (cw_ref[0:1, cols] * sh2 + cw_ref[1:2, cols] * sh1 + cw_ref[2:3, cols] * u)
        y_ref[:, cols] = y.astype(BF16)
        if carried:
            tail = u[bm - V7X_SUBLANES:bm, :]
            carry_scr[:, cols] = tail
            tail_ref[:, cols] = tail
        else:
            tail_ref[:, cols] = u


def _conv_proj(xg, ssq, w_in, conv_w, prev, *, layer, bm, bc, seq):
    M, D = xg.shape
    C = conv_w.shape[2]
    col0 = (w_in.shape[2] - 3 * C) // bc
    w_spec = lambda j: pl.BlockSpec((None, D, bc), lambda c, m: (layer, 0, col0 + j * (C // bc) + c),
                                    pipeline_mode=_weight_buffering(M // bm))
    tile = pl.BlockSpec((bm, bc), lambda c, m: (m, c))
    in_specs = [
        pl.BlockSpec((bm, D), lambda c, m: (m, 0)),
        pl.BlockSpec((bm, ssq.shape[1]), lambda c, m: (m, 0)),
        w_spec(0), w_spec(1), w_spec(2),
        pl.BlockSpec((None, CONV_K, bc), lambda c, m: (layer, 0, c)),
    ]
    scratch = [pltpu.VMEM((D, bc), BF16)] * 3
    if prev is None:
        tiles_per_seq = seq // bm
        batch = M // seq
        tail_shape = jax.ShapeDtypeStruct((batch, V7X_SUBLANES, C), F32)
        tail_spec = pl.BlockSpec((None, V7X_SUBLANES, bc), lambda c, m: (m // tiles_per_seq, 0, c))
        scratch = scratch + [pltpu.VMEM((V7X_SUBLANES, bc), F32)]
        operands = ()
    else:
        tiles_per_seq = None
        tail_shape = jax.ShapeDtypeStruct((M, C), F32)
        tail_spec = tile
        in_specs = in_specs + [tile, tile]
        operands = prev
    block_bytes = 2 * (_nbytes((bm, D), BF16) + 4 * _nbytes((bm, bc), F32)) \
        + 3 * _nbytes((D, bc), F32) + 3 * _nbytes((D, bc), BF16) + 8 * _nbytes((bm, MXU_WIDTH), F32)
    return pl.pallas_call(
        functools.partial(_conv_proj_kernel, seq=seq, tiles_per_seq=tiles_per_seq),
        out_shape=[jax.ShapeDtypeStruct((M, C), BF16), tail_shape],
        grid=(C // bc, M // bm),
        in_specs=in_specs,
        out_specs=[tile, tail_spec],
        scratch_shapes=scratch,
        compiler_params=_compiler_params(("arbitrary", "arbitrary"), block_bytes),
        name="conv_proj",
    )(xg, ssq, w_in, w_in, w_in, conv_w, *operands)


def _sample_conv_prev(state, seq):
    DB, _, C = state.shape
    zeros = jnp.zeros((DB, seq - 1, C), F32)
    prev1 = jnp.concatenate([state[:, 1:2], zeros], axis=1).reshape(DB * seq, C)
    prev2 = jnp.concatenate([state, zeros[:, 1:]], axis=1).reshape(DB * seq, C)
    return prev1, prev2


def _prompt_query_block(qi, lam, g_ref, q_ref, k_ref, v_ref, o_ref, qs_scr, p_scr, *, blk, lam_init):
    E = q_ref.shape[1]
    half = E // 2
    lane = lax.broadcasted_iota(jnp.int32, (blk, E), 1)
    row = lax.broadcasted_iota(jnp.int32, (2 * blk, blk), 0)
    col = lax.broadcasted_iota(jnp.int32, (2 * blk, blk), 1)
    causal = col <= jnp.where(row >= blk, row - blk, row)

    def scores(c):
        s = lax.dot_general(qs_scr[...], k_ref[c * blk:(c + 1) * blk, :], NT_DIMS, preferred_element_type=F32)
        return jnp.where(causal, s, MASK_VALUE) if c == qi else s

    q = q_ref[qi * blk:(qi + 1) * blk, :]
    qs_scr[0:blk, :] = jnp.where(lane < half, q, jnp.zeros_like(q))
    qs_scr[blk:2 * blk, :] = jnp.where(lane >= half, q, jnp.zeros_like(q))
    m_lanes = jnp.full((2 * blk, V7X_LANES), MASK_VALUE, F32)
    for c in range(qi + 1):
        m_lanes = jnp.maximum(m_lanes, _lane_fold(scores(c), jnp.maximum))
    m = jnp.broadcast_to(jnp.max(m_lanes, axis=1, keepdims=True), (2 * blk, V7X_LANES))
    l_lanes = jnp.zeros((2 * blk, V7X_LANES), F32)
    for c in range(qi + 1):
        s = scores(c)
        for j in range(blk // V7X_LANES):
            p = jnp.exp2(s[:, j * V7X_LANES:(j + 1) * V7X_LANES] - m)
            l_lanes = l_lanes + p
            p_scr[:, c * blk + j * V7X_LANES:c * blk + (j + 1) * V7X_LANES] = p.astype(BF16)
    l = jnp.sum(l_lanes, axis=1, keepdims=True)
    visible = (qi + 1) * blk
    acc = jnp.dot(p_scr[:, 0:visible], v_ref[0:visible, :], preferred_element_type=F32)
    o = _head_finish(acc, l, lam, g_ref[...], 1.0 - lam_init, blk)
    o_ref[qi * blk:(qi + 1) * blk, :] = o.astype(o_ref.dtype)


def _sample_start(q_ref, kn_ref, vn_ref, qall_scr, bias_scr, m_scr, l_scr, acc_scr):
    S, H, E = q_ref.shape
    half = E // 2
    rr = lax.broadcasted_iota(jnp.int32, bias_scr.shape, 0)
    cc = lax.broadcasted_iota(jnp.int32, bias_scr.shape, 1)
    bias_scr[...] = jnp.where(cc % H == rr % H, 0.0, MASK_VALUE)
    lane = lax.broadcasted_iota(jnp.int32, (H, E), 1)
    for m in range(2):
        for s in range(S):
            rows = pl.ds((m * S + s) * H, H)
            q = jnp.where(lane // half == m, q_ref[s], 0.0)
            qall_scr[rows, :] = q.astype(BF16)
            sc = [jnp.sum(q * kn_ref[j], axis=1, keepdims=True) for j in range(s + 1)]
            m0 = functools.reduce(jnp.maximum, sc)
            ps = [jnp.exp2(c - m0) for c in sc]
            m_scr[rows, :] = m0
            l_scr[rows, :] = functools.reduce(jnp.add, ps)
            acc_scr[rows, :] = functools.reduce(jnp.add, [p * vn_ref[j] for j, p in enumerate(ps)])


def _sample_pages(k_refs, v_refs, qall_scr, bias_scr, m_scr, l_scr, acc_scr):
    qall = qall_scr[...]
    bias = bias_scr[...]
    scores = []
    for k_ref in k_refs:
        kj = k_ref[...].astype(BF16)
        scores.append(lax.dot_general(qall, kj, NT_DIMS, preferred_element_type=F32) + bias)
    m_old = m_scr[...]
    m_new = m_old
    for s in scores:
        m_new = jnp.maximum(m_new, jnp.max(s, axis=1, keepdims=True))
    alpha = jnp.exp2(m_old - m_new)
    l_new = alpha * l_scr[...]
    acc = alpha * acc_scr[...]
    for s, v_ref in zip(scores, v_refs):
        p = jnp.exp2(s - m_new)
        l_new = l_new + jnp.sum(p, axis=1, keepdims=True)
        acc = acc + jnp.dot(p.astype(BF16), v_ref[...].astype(BF16), preferred_element_type=F32)
    m_scr[...] = m_new
    l_scr[...] = l_new
    acc_scr[...] = acc


def _attention_kernel(pt_ref, lamv_ref, g_ref, q_ref, k_ref, v_ref, sq_ref, kn_ref, vn_ref, *refs,
                      blk, pages_per_step, steps_per_seq, lam_init):
    P = pages_per_step
    kp_refs, vp_refs = refs[:P], refs[P:2 * P]
    o_ref, so_ref, qs_scr, p_scr, qall_scr, bias_scr, m_scr, l_scr, acc_scr = refs[2 * P:]
    sample_state = (qall_scr, bias_scr, m_scr, l_scr, acc_scr)
    S, H, E = sq_ref.shape
    n_blocks = q_ref.shape[0] // blk
    n_sub = n_blocks // 2
    step = pl.program_id(0)
    lam = _lambda_value(lamv_ref, lam_init)

    @pl.when(step % steps_per_seq == 0)
    def _():
        _sample_start(sq_ref, kn_ref, vn_ref, *sample_state)

    for j in range(n_sub):
        @pl.when(step % n_sub == j)
        def _(j=j):
            _sample_pages(kp_refs, vp_refs, *sample_state)
            for qi in (j, n_blocks - 1 - j):
                _prompt_query_block(qi, lam, g_ref, q_ref, k_ref, v_ref, o_ref, qs_scr, p_scr,
                                    blk=blk, lam_init=lam_init)

    @pl.when(step % steps_per_seq == steps_per_seq - 1)
    def _():
        o = _head_finish(acc_scr[...], l_scr[...], lam, g_ref[...], 1.0 - lam_init, S * H)
        so_ref[...] = o.reshape(S, H, E)


def _attention(qkv, sq, k_new, v_new, cache_k, cache_v, page_table, lamv, g, *, layer, batch, blk, lam_init):
    M = qkv.shape[0]
    DB, S, H, E = sq.shape
    T = M // batch
    page_rows = cache_k.shape[2]
    n_pages = page_table.shape[1]
    n_sub = T // blk // 2
    n_steps = batch * H * n_sub
    assert T % (2 * blk) == 0 and (DB * n_pages) % n_steps == 0, "the page stream must split evenly over the steps"
    P = DB * n_pages // n_steps
    assert n_pages % P == 0
    steps_per_seq = n_pages // P
    R = 2 * S * H
    head = lambda offset: pl.BlockSpec((T, E), lambda i, pt: (i // (H * n_sub), offset + (i // n_sub) % H))
    per_seq = pl.BlockSpec((None, S, H, E), lambda i, pt: (i // steps_per_seq, 0, 0, 0))

    def page_spec(j):
        return pl.BlockSpec((None, None, page_rows, E),
                            lambda i, pt: (layer, pt[i // steps_per_seq, (i % steps_per_seq) * P + j], 0, 0))

    block_bytes = 2 * (2 * P * _nbytes((page_rows, E), F32) + 4 * _nbytes((T, E), BF16)) \
        + _nbytes((2 * blk, T), BF16) + 3 * _nbytes((2 * blk, blk), F32) + (2 + P) * _nbytes((R, page_rows), F32)
    return pl.pallas_call(
        functools.partial(_attention_kernel, blk=blk, pages_per_step=P, steps_per_seq=steps_per_seq,
                          lam_init=lam_init),
        out_shape=[jax.ShapeDtypeStruct((M, H * E), BF16), jax.ShapeDtypeStruct((DB, S, H, E), F32)],
        grid_spec=pltpu.PrefetchScalarGridSpec(
            num_scalar_prefetch=1,
            grid=(n_steps,),
            in_specs=[
                pl.BlockSpec(lamv.shape, lambda i, pt: (0, 0)),
                pl.BlockSpec((1, E), lambda i, pt: (0, 0)),
                head(0), head(H), head(2 * H),
                per_seq, per_seq, per_seq,
            ] + [page_spec(j) for j in range(P)] + [page_spec(j) for j in range(P)],
            out_specs=[head(0), per_seq],
            scratch_shapes=[
                pltpu.VMEM((2 * blk, E), BF16), pltpu.VMEM((2 * blk, T), BF16),
                pltpu.VMEM((R, E), BF16), pltpu.VMEM((R, page_rows), F32),
                pltpu.VMEM((R, 1), F32), pltpu.VMEM((R, 1), F32), pltpu.VMEM((R, E), F32),
            ],
        ),
        compiler_params=_compiler_params(("arbitrary",), block_bytes),
        name="attention",
    )(page_table, lamv, g.reshape(1, E), qkv, qkv, qkv, sq, k_new, v_new, *([cache_k] * P), *([cache_v] * P))


def _residual_tile(lhs_refs, w_refs, r_ref, g_ref, x_ref, xg_ref, ssq_ref):
    bn = x_ref.shape[1]
    bw = min(bn, MXU_WIDTH)
    ssq = None
    for c in range(bn // bw):
        cols = slice(c * bw, (c + 1) * bw)
        x = r_ref[:, cols]
        for lhs_ref, w_ref in zip(lhs_refs, w_refs):
            x = x + jnp.dot(lhs_ref[...], w_ref[:, cols], preferred_element_type=F32)
        x_ref[:, cols] = x
        xg_ref[:, cols] = (x * g_ref[:, cols]).astype(BF16)
        part = _lane_fold(x * x, jnp.add)
        ssq = part if ssq is None else ssq + part
    ssq_ref[...] = ssq


def _out_proj_kernel(a_ref, y_ref, w_ref, r_ref, g_ref, x_ref, xg_ref, ssq_ref, w_scr):
    @pl.when(pl.program_id(0) == 0)
    def _():
        w_scr[...] = w_ref[...].astype(BF16)

    ka = a_ref.shape[1]
    _residual_tile((a_ref, y_ref), (w_scr.at[0:ka, :], w_scr.at[ka:, :]), r_ref, g_ref, x_ref, xg_ref, ssq_ref)


def _out_proj(a, y, w, res, g, *, layer, bm):
    M, ka = a.shape
    K = ka + y.shape[1]
    N = w.shape[2]
    rows = lambda width: pl.BlockSpec((bm, width), lambda m: (m, 0))
    block_bytes = 2 * (_nbytes((bm, K), BF16) + 2 * _nbytes((bm, N), F32) + _nbytes((bm, N), BF16)) \
        + _nbytes((K, N), F32) + _nbytes((K, N), BF16) + 4 * _nbytes((bm, MXU_WIDTH), F32)
    return pl.pallas_call(
        _out_proj_kernel,
        out_shape=[jax.ShapeDtypeStruct((M, N), F32), jax.ShapeDtypeStruct((M, N), BF16),
                   jax.ShapeDtypeStruct((M, V7X_LANES), F32)],
        grid=(M // bm,),
        in_specs=[
            rows(ka), rows(K - ka),
            pl.BlockSpec((None, K, N), lambda m: (layer, 0, 0), pipeline_mode=pl.Buffered(1)),
            rows(N),
            pl.BlockSpec((1, N), lambda m: (0, 0)),
        ],
        out_specs=[rows(N), rows(N), rows(V7X_LANES)],
        scratch_shapes=[pltpu.VMEM((K, N), BF16)],
        compiler_params=_compiler_params(("arbitrary",), block_bytes),
        name="out_proj",
    )(a, y, w, res, g.reshape(1, N))


def _swiglu_tile(xg_ref, ssq_ref, wg_scr, wu_scr, o_ref):
    r = _row_rsqrt(ssq_ref, xg_ref.shape[1])
    bn = o_ref.shape[1]
    bw = min(bn, MXU_WIDTH)
    for c in range(bn // bw):
        cols = slice(c * bw, (c + 1) * bw)
        gate = jnp.dot(xg_ref[...], wg_scr[:, cols], preferred_element_type=F32) * r
        up = jnp.dot(xg_ref[...], wu_scr[:, cols], preferred_element_type=F32) * r
        o_ref[:, cols] = ((gate / (1.0 + jnp.exp(-gate))) * up).astype(BF16)


def _gate_up_kernel(xg_ref, ssq_ref, xsg_ref, sssq_ref, wg_ref, wu_ref, wd_ref,
                    o_ref, os_ref, wdb_ref, wg_scr, wu_scr):
    @pl.when(pl.program_id(1) == 0)
    def _():
        wg_scr[...] = wg_ref[...].astype(BF16)
        wu_scr[...] = wu_ref[...].astype(BF16)
        _swiglu_tile(xsg_ref, sssq_ref, wg_scr, wu_scr, os_ref)

    wdb_ref[...] = wd_ref[...].astype(BF16)
    _swiglu_tile(xg_ref, ssq_ref, wg_scr, wu_scr, o_ref)


def _gate_up(xg, ssq, xsg, sssq, wg, wu, wd, *, layer, bm, bn):
    M, D = xg.shape
    Ms = xsg.shape[0]
    F, N = wd.shape[1:]
    n_m = M // bm
    slab = F // (F // bn * n_m)
    w_spec = pl.BlockSpec((None, D, bn), lambda n, m: (layer, 0, n))
    block_bytes = 2 * (_nbytes((bm, D), BF16) + 2 * _nbytes((D, bn), F32) + _nbytes((bm, bn), BF16)
                       + _nbytes((slab, N), F32) + _nbytes((slab, N), BF16) + _nbytes((Ms, D), BF16)) \
        + 2 * _nbytes((D, bn), BF16) + 6 * _nbytes((bm, MXU_WIDTH), F32)
    return pl.pallas_call(
        _gate_up_kernel,
        out_shape=[jax.ShapeDtypeStruct((M, F), BF16), jax.ShapeDtypeStruct((Ms, F), BF16),
                   jax.ShapeDtypeStruct((F, N), BF16)],
        grid=(F // bn, n_m),
        in_specs=[
            pl.BlockSpec((bm, D), lambda n, m: (m, 0)),
            pl.BlockSpec((bm, ssq.shape[1]), lambda n, m: (m, 0)),
            pl.BlockSpec((Ms, D), lambda n, m: (0, 0)),
            pl.BlockSpec((Ms, sssq.shape[1]), lambda n, m: (0, 0)),
            w_spec, w_spec,
            pl.BlockSpec((None, slab, N), lambda n, m: (layer, n * n_m + m, 0)),
        ],
        out_specs=[
            pl.BlockSpec((bm, bn), lambda n, m: (m, n)),
            pl.BlockSpec((Ms, bn), lambda n, m: (0, n)),
            pl.BlockSpec((slab, N), lambda n, m: (n * n_m + m, 0)),
        ],
        scratch_shapes=[pltpu.VMEM((D, bn), BF16)] * 2,
        compiler_params=_compiler_params(("arbitrary", "arbitrary"), block_bytes),
        name="gate_up",
    )(xg, ssq, xsg, sssq, wg, wu, wd)


def _down_kernel(a_ref, w_ref, r_ref, g_ref, x_ref, xg_ref, ssq_ref):
    _residual_tile((a_ref,), (w_ref,), r_ref, g_ref, x_ref, xg_ref, ssq_ref)


def _down(a, w, res, g, *, bm, bn):
    M, F = a.shape
    N = w.shape[1]
    tile = lambda width: pl.BlockSpec((bm, width), lambda m, n: (m, n))
    block_bytes = 2 * (_nbytes((bm, F), BF16) + _nbytes((F, bn), BF16) + 2 * _nbytes((bm, bn), F32)
                       + _nbytes((bm, bn), BF16)) + 4 * _nbytes((bm, MXU_WIDTH), F32)
    return pl.pallas_call(
        _down_kernel,
        out_shape=[jax.ShapeDtypeStruct((M, N), F32), jax.ShapeDtypeStruct((M, N), BF16),
                   jax.ShapeDtypeStruct((M, N // bn * V7X_LANES), F32)],
        grid=(M // bm, N // bn),
        in_specs=[
            pl.BlockSpec((bm, F), lambda m, n: (m, 0)),
            pl.BlockSpec((F, bn), lambda m, n: (0, n)),
            tile(bn),
            pl.BlockSpec((1, bn), lambda m, n: (0, n)),
        ],
        out_specs=[tile(bn), tile(bn), tile(V7X_LANES)],
        compiler_params=_compiler_params(("arbitrary", "arbitrary"), block_bytes),
        name="down_proj",
    )(a, w, res, g.reshape(1, N))


def _final_norm_kernel(x_ref, g_ref, o_ref):
    o_ref[...] = _rms_scale(x_ref[...], RMS_EPS) * g_ref[...]


def _final_norm(x, g, *, bm):
    M, D = x.shape
    rows = pl.BlockSpec((bm, D), lambda m: (m, 0))
    return pl.pallas_call(
        _final_norm_kernel,
        out_shape=jax.ShapeDtypeStruct((M, D), F32),
        grid=(M // bm,),
        in_specs=[rows, pl.BlockSpec((1, D), lambda m: (0, 0))],
        out_specs=rows,
        compiler_params=_compiler_params(("arbitrary",), 4 * _nbytes((bm, D), F32)),
        name="final_norm",
    )(x, g.reshape(1, D))


def _pick_block(size, preferred):
    b = min(size, preferred)
    while size % b:
        b //= 2
    return b


def kernel(x_prompt, x_sample, cache_k, cache_v, state_conv, page_table, attn_norm, w_in, conv_w,
           lambda_q1, lambda_k1, lambda_q2, lambda_k2, subln_g, w_out, ffn_norm, w_gate, w_up,
           w_down, final_norm):
    B, T, D = x_prompt.shape
    DB, S, _ = x_sample.shape
    depth, pool, page, H, E = cache_k.shape
    assert E == V7X_LANES and H == V7X_SUBLANES, "one (head, value) tile per position is assumed"
    head_dim = E // 2
    W = H * E
    C = conv_w.shape[2]
    past = page_table.shape[1] * page
    Mp, Ms = B * T, DB * S

    cos_p, sin_p = _rope_tables(np.arange(T), head_dim)
    cos_s, sin_s = _rope_tables(np.tile(past + np.arange(S), DB), head_dim)
    ck = cache_k.reshape(depth, pool, page * H, E)
    cv = cache_v.reshape(depth, pool, page * H, E)

    bm_attn = _pick_block(T, 1024)
    bm_conv = _pick_block(T, 1024)
    bm_out = _pick_block(T, 512)
    bm_ff = _pick_block(T, 1024)
    bm_down = _pick_block(T, 1024)
    blk = _pick_block(T, 512)
    bc = _pick_block(C, 512)
    bn = _pick_block(D, 512)
    bn_ff = _pick_block(w_gate.shape[2], 512)

    xp = x_prompt.reshape(Mp, D)
    xs = x_sample.reshape(Ms, D)
    xpg, pssq = _prep(xp, attn_norm[0], bm=bm_ff)
    xsg, sssq = _prep(xs, attn_norm[0], bm=Ms)
    kv_prompt = (jnp.zeros((depth, Mp * H, E), F32), jnp.zeros((depth, Mp * H, E), F32))
    cp_l, ks_l, vs_l, cs_l = [], [], [], []
    for l in range(depth):
        lam_init = 0.8 - 0.6 * math.exp(-0.3 * l)
        lamv = jnp.stack([lambda_q1[l], lambda_k1[l], lambda_q2[l], lambda_k2[l]])
        g_next = attn_norm[l + 1] if l + 1 < depth else final_norm

        qkv, k_all, v_all = _attn_proj(xpg, pssq, w_in, cos_p, sin_p, kv_prompt, layer=l, slab=l,
                                       n_slabs=depth, bm=bm_attn, n_heads=H)
        kv_prompt = (k_all, v_all)
        yc, tail = _conv_proj(xpg, pssq, w_in, conv_w, None, layer=l, bm=bm_conv, bc=bc, seq=T)
        cp_l.append(tail[:, V7X_SUBLANES - (CONV_K - 1):])

        sqkv, k_new, v_new = _attn_proj(xsg, sssq, w_in, cos_s, sin_s, None, layer=l, slab=0, n_slabs=1,
                                        bm=Ms, n_heads=H)
        k_new = k_new.reshape(DB, S, H, E)
        v_new = v_new.reshape(DB, S, H, E)
        ycs, u = _conv_proj(xsg, sssq, w_in, conv_w, _sample_conv_prev(state_conv[l], S),
                            layer=l, bm=Ms, bc=bc, seq=S)

        ao, aos = _attention(qkv, sqkv[:, :W].astype(F32).reshape(DB, S, H, E), k_new, v_new, ck, cv,
                             page_table, lamv, subln_g[l], layer=l, batch=B, blk=blk, lam_init=lam_init)
        xp, xpg, pssq = _out_proj(ao, yc, w_out, xp, ffn_norm[l], layer=l, bm=bm_out)
        xs, xsg, sssq = _out_proj(aos.reshape(Ms, W).astype(BF16), ycs, w_out, xs, ffn_norm[l], layer=l, bm=Ms)
        act_p, act_s, wd = _gate_up(xpg, pssq, xsg, sssq, w_gate, w_up, w_down, layer=l, bm=bm_ff, bn=bn_ff)
        xp, xpg, pssq = _down(act_p, wd, xp, g_next, bm=bm_down, bn=bn)
        xs, xsg, sssq = _down(act_s, wd, xs, g_next, bm=Ms, bn=bn)
        ks_l.append(k_new)
        vs_l.append(v_new)
        cs_l.append(u.reshape(DB, S, C)[:, S - (CONV_K - 1):])

    y_prompt = _final_norm(xp, final_norm, bm=bm_out).reshape(B, T, D)
    y_sample = _final_norm(xs, final_norm, bm=Ms).reshape(DB, S, D)
    k_prompt, v_prompt = (a.reshape(depth, B, T, H, E) for a in kv_prompt)
    return (y_prompt, y_sample, k_prompt, v_prompt, jnp.stack(cp_l),
            jnp.stack(ks_l), jnp.stack(vs_l), jnp.stack(cs_l))
```

```python
import functools
import math

import numpy as np
import jax
import jax.numpy as jnp
from jax import lax
from jax.experimental import pallas as pl
from jax.experimental.pallas import tpu as pltpu

ROPE_THETA = 10000.0
RMS_EPS = 1e-6
SUBLN_EPS = 1e-5
CONV_K = 3
MASK_VALUE = -1e30
SAMPLE_PAGE_GROUP = 8
LOG2_E = math.log2(math.e)

V7X_LANES = 128
V7X_SUBLANES = 8
V7X_VMEM_BYTES = 64 * 1024 * 1024
MXU_WIDTH = 256
VMEM_TEMP_BYTES = 12 * 1024 * 1024

F32 = jnp.float32
BF16 = jnp.bfloat16
NT_DIMS = (((1,), (1,)), ((), ()))


def _compiler_params(semantics, block_bytes):
    limit = min(block_bytes + VMEM_TEMP_BYTES, V7X_VMEM_BYTES - 6 * 1024 * 1024)
    return pltpu.CompilerParams(dimension_semantics=semantics, vmem_limit_bytes=int(limit))


def _weight_buffering(n_row_tiles):
    return pl.Buffered(1) if n_row_tiles > 1 else None


def _nbytes(shape, dtype):
    return int(np.prod(shape)) * jnp.dtype(dtype).itemsize


def _rms_scale(x, eps):
    return x * lax.rsqrt(jnp.mean(x * x, axis=-1, keepdims=True) + eps)


def _row_rsqrt(ssq_ref, width):
    return lax.rsqrt(jnp.sum(ssq_ref[...], axis=1, keepdims=True) * (1.0 / width) + RMS_EPS)


def _lane_fold(x, op):
    return functools.reduce(op, [x[:, j * V7X_LANES:(j + 1) * V7X_LANES] for j in range(x.shape[1] // V7X_LANES)])


def _emit_norm_inputs(x, g_ref, xg_ref, ssq_ref):
    xg_ref[...] = (x * g_ref[...]).astype(BF16)
    ssq_ref[...] = _lane_fold(x * x, jnp.add)


def _lambda_value(lamv_ref, lam_init):
    d1 = jnp.sum(lamv_ref[0:1, :] * lamv_ref[1:2, :], axis=1, keepdims=True)
    d2 = jnp.sum(lamv_ref[2:3, :] * lamv_ref[3:4, :], axis=1, keepdims=True)
    return jnp.exp(d1) - jnp.exp(d2) + lam_init


def _head_finish(acc, l, lam, g, out_scale, rows):
    o = acc[0:rows] / l[0:rows] - lam * (acc[rows:2 * rows] / l[rows:2 * rows])
    return (_rms_scale(o, SUBLN_EPS) * g) * out_scale


def _prep_kernel(x_ref, g_ref, xg_ref, ssq_ref):
    _emit_norm_inputs(x_ref[...], g_ref, xg_ref, ssq_ref)


def _prep(x, g, *, bm):
    M, D = x.shape
    rows = lambda width: pl.BlockSpec((bm, width), lambda m: (m, 0))
    return pl.pallas_call(
        _prep_kernel,
        out_shape=[jax.ShapeDtypeStruct((M, D), BF16), jax.ShapeDtypeStruct((M, V7X_LANES), F32)],
        grid=(M // bm,),
        in_specs=[rows(D), pl.BlockSpec((1, D), lambda m: (0, 0))],
        out_specs=[rows(D), rows(V7X_LANES)],
        compiler_params=_compiler_params(("arbitrary",), 3 * _nbytes((bm, D), F32)),
        name="prep",
    )(x, g.reshape(1, D))


def _rope_tables(positions, head_dim):
    inv = 1.0 / (ROPE_THETA ** (np.arange(0, head_dim, 2, dtype=np.float64) / head_dim))
    ang = np.asarray(positions, np.float64)[:, None] * inv[None, :]
    ang = np.concatenate([ang, ang], axis=-1)
    sign = np.concatenate([-np.ones(head_dim // 2), np.ones(head_dim // 2)])
    reps = V7X_LANES // head_dim
    cos = np.tile(np.cos(ang), (1, reps)).astype(np.float32)
    sin = np.tile(np.sin(ang) * sign[None, :], (1, reps)).astype(np.float32)
    return jnp.asarray(cos), jnp.asarray(sin)


def _rope(zc, cos, sin, low, half):
    ahead = pltpu.roll(zc, V7X_LANES - half, axis=1)
    behind = pltpu.roll(zc, half, axis=1)
    return zc * cos + jnp.where(low, ahead, behind) * sin


def _attn_proj_kernel(xg_ref, ssq_ref, w_ref, cos_ref, sin_ref, *refs, half, q_scale):
    qkv_ref, k_ref, v_ref, w_scr = refs[-4:]
    n = pl.program_id(0)
    bm, W = qkv_ref.shape
    E = k_ref.shape[1]
    H = W // E

    @pl.when(pl.program_id(1) == 0)
    def _():
        w_scr[...] = w_ref[...].astype(BF16)

    r = _row_rsqrt(ssq_ref, xg_ref.shape[1])
    lane = lax.broadcasted_iota(jnp.int32, (bm, E), 1)
    low = (lane % (2 * half)) < half

    def heads():
        per_group = MXU_WIDTH // E
        for c in range(H // per_group):
            cols = slice(c * MXU_WIDTH, (c + 1) * MXU_WIDTH)
            z = jnp.dot(xg_ref[...], w_scr[:, cols], preferred_element_type=F32) * r
            for j in range(per_group):
                yield c * per_group + j, z[:, j * E:(j + 1) * E]

    @pl.when(n == 0)
    def _():
        for h, z in heads():
            q = _rope(z, cos_ref[...], sin_ref[...], low, half)
            qkv_ref[:, h * E:(h + 1) * E] = (q * q_scale).astype(BF16)

    @pl.when(n == 1)
    def _():
        for h, z in heads():
            k = _rope(z, cos_ref[...], sin_ref[...], low, half)
            qkv_ref[:, h * E:(h + 1) * E] = k.astype(BF16)
            k_ref[pl.ds(h, bm, stride=H), :] = k

    @pl.when(n == 2)
    def _():
        for h, z in heads():
            qkv_ref[:, h * E:(h + 1) * E] = z.astype(BF16)
            v_ref[pl.ds(h, bm, stride=H), :] = z


def _attn_proj(xg, ssq, w_in, cos, sin, kv_all, *, layer, slab, n_slabs, bm, n_heads):
    M, D = xg.shape
    W = w_in.shape[2] // 6
    H = n_heads
    E = W // H
    head_dim = E // 2
    n_row_tiles = M // bm
    n_pos_blocks = cos.shape[0] // bm
    k_rows = lambda n, m: (slab, jnp.where(n == 1, m, jnp.where(n < 1, 0, n_row_tiles - 1)), 0)
    v_rows = lambda n, m: (slab, jnp.where(n == 2, m, 0), 0)
    stacked = jax.ShapeDtypeStruct((n_slabs, M * H, E), F32)
    aliases = {} if kv_all is None else {5: 1, 6: 2}
    extra_in = [] if kv_all is None else [pl.BlockSpec(memory_space=pl.ANY)] * 2
    block_bytes = 2 * (_nbytes((bm, D), BF16) + _nbytes((bm, W), BF16) + 2 * _nbytes((bm, W), F32)) \
        + _nbytes((D, W), F32) + _nbytes((D, W), BF16) + 4 * _nbytes((bm, MXU_WIDTH), F32)
    return pl.pallas_call(
        functools.partial(_attn_proj_kernel, half=head_dim // 2, q_scale=head_dim ** -0.5 * LOG2_E),
        out_shape=[jax.ShapeDtypeStruct((M, 3 * W), BF16), stacked, stacked],
        grid=(3, n_row_tiles),
        in_specs=[
            pl.BlockSpec((bm, D), lambda n, m: (m, 0)),
            pl.BlockSpec((bm, ssq.shape[1]), lambda n, m: (m, 0)),
            pl.BlockSpec((None, D, W), lambda n, m: (layer, 0, n), pipeline_mode=_weight_buffering(n_row_tiles)),
            pl.BlockSpec((bm, V7X_LANES), lambda n, m: (m % n_pos_blocks, 0)),
            pl.BlockSpec((bm, V7X_LANES), lambda n, m: (m % n_pos_blocks, 0)),
        ] + extra_in,
        out_specs=[
            pl.BlockSpec((bm, W), lambda n, m: (m, n)),
            pl.BlockSpec((None, bm * H, E), k_rows),
            pl.BlockSpec((None, bm * H, E), v_rows),
        ],
        scratch_shapes=[pltpu.VMEM((D, W), BF16)],
        input_output_aliases=aliases,
        compiler_params=_compiler_params(("arbitrary", "arbitrary"), block_bytes),
        name="attn_proj",
    )(xg, ssq, w_in, cos, sin, *([] if kv_all is None else kv_all))


def _conv_proj_kernel(xg_ref, ssq_ref, wb_ref, wc_ref, wh_ref, cw_ref, *refs, seq, tiles_per_seq):
    carried = tiles_per_seq is not None
    if carried:
        y_ref, tail_ref, wb_scr, wc_scr, wh_scr, carry_scr = refs
    else:
        prev1_ref, prev2_ref, y_ref, tail_ref, wb_scr, wc_scr, wh_scr = refs
    m = pl.program_id(1)
    bm, bc = y_ref.shape

    @pl.when(m == 0)
    def _():
        wb_scr[...] = wb_ref[...].astype(BF16)
        wc_scr[...] = wc_ref[...].astype(BF16)
        wh_scr[...] = wh_ref[...].astype(BF16)

    r = _row_rsqrt(ssq_ref, xg_ref.shape[1])
    if carried:
        @pl.when(m % tiles_per_seq == 0)
        def _():
            carry_scr[...] = jnp.zeros(carry_scr.shape, F32)

    bw = min(bc, MXU_WIDTH)
    t = lax.broadcasted_iota(jnp.int32, (bm, bw), 0)
    for c in range(bc // bw):
        cols = slice(c * bw, (c + 1) * bw)
        gb = jnp.dot(xg_ref[...], wb_scr[:, cols], preferred_element_type=F32) * r
        gc = jnp.dot(xg_ref[...], wc_scr[:, cols], preferred_element_type=F32) * r
        hc = jnp.dot(xg_ref[...], wh_scr[:, cols], preferred_element_type=F32) * r
        u = gc * hc
        if carried:
            last = jnp.broadcast_to(carry_scr[V7X_SUBLANES - 1:V7X_SUBLANES, cols], (bm, bw))
            second_last = jnp.broadcast_to(carry_scr[V7X_SUBLANES - 2:V7X_SUBLANES - 1, cols], (bm, bw))
            prev1 = last
            prev2 = jnp.where(t == 0, second_last, last)
            pos = t
        else:
            prev1 = prev1_ref[:, cols]
            prev2 = prev2_ref[:, cols]
            pos = t % seq
        sh1 = jnp.where(pos >= 1, pltpu.roll(u, 1, axis=0), prev1)
        sh2 = jnp.where(pos >= 2, pltpu.roll(u, 2, axis=0), prev2)
        y = gb * (cw_ref[0:1, cols] * sh2 + cw_ref[1:2, cols] * sh1 + cw_ref[2:3, cols] * u)
        y_ref[:, cols] = y.astype(BF16)
        if carried:
            tail = u[bm - V7X_SUBLANES:bm, :]
            carry_scr[:, cols] = tail
            tail_ref[:, cols] = tail
        else:
            tail_ref[:, cols] = u


def _conv_proj(xg, ssq, w_in, conv_w, prev, *, layer, bm, bc, seq):
    M, D = xg.shape
    C = conv_w.shape[2]
    col0 = (w_in.shape[2] - 3 * C) // bc
    w_spec = lambda j: pl.BlockSpec((None, D, bc), lambda c, m: (layer, 0, col0 + j * (C // bc) + c),
                                    pipeline_mode=_weight_buffering(M // bm))
    tile = pl.BlockSpec((bm, bc), lambda c, m: (m, c))
    in_specs = [
        pl.BlockSpec((bm, D), lambda c, m: (m, 0)),
        pl.BlockSpec((bm, ssq.shape[1]), lambda c, m: (m, 0)),
        w_spec(0), w_spec(1), w_spec(2),
        pl.BlockSpec((None, CONV_K, bc), lambda c, m: (layer, 0, c)),
    ]
    scratch = [pltpu.VMEM((D, bc), BF16)] * 3
    if prev is None:
        tiles_per_seq = seq // bm
        batch = M // seq
        tail_shape = jax.ShapeDtypeStruct((batch, V7X_SUBLANES, C), F32)
        tail_spec = pl.BlockSpec((None, V7X_SUBLANES, bc), lambda c, m: (m // tiles_per_seq, 0, c))
        scratch = scratch + [pltpu.VMEM((V7X_SUBLANES, bc), F32)]
        operands = ()
    else:
        tiles_per_seq = None
        tail_shape = jax.ShapeDtypeStruct((M, C), F32)
        tail_spec = tile
        in_specs = in_specs + [tile, tile]
        operands = prev
    block_bytes = 2 * (_nbytes((bm, D), BF16) + 4 * _nbytes((bm, bc), F32)) \
        + 3 * _nbytes((D, bc), F32) + 3 * _nbytes((D, bc), BF16) + 8 * _nbytes((bm, MXU_WIDTH), F32)
    return pl.pallas_call(
        functools.partial(_conv_proj_kernel, seq=seq, tiles_per_seq=tiles_per_seq),
        out_shape=[jax.ShapeDtypeStruct((M, C), BF16), tail_shape],
        grid=(C // bc, M // bm),
        in_specs=in_specs,
        out_specs=[tile, tail_spec],
        scratch_shapes=scratch,
        compiler_params=_compiler_params(("arbitrary", "arbitrary"), block_bytes),
        name="conv_proj",
    )(xg, ssq, w_in, w_in, w_in, conv_w, *operands)


def _sample_conv_prev(state, seq):
    DB, _, C = state.shape
    zeros = jnp.zeros((DB, seq - 1, C), F32)
    prev1 = jnp.concatenate([state[:, 1:2], zeros], axis=1).reshape(DB * seq, C)
    prev2 = jnp.concatenate([state, zeros[:, 1:]], axis=1).reshape(DB * seq, C)
    return prev1, prev2


def _prompt_query_block(qi, lam, g_col_ref, q_ref, k_ref, vt_scr, o_ref, *, blk, lam_init):
    E = q_ref.shape[1]
    half = E // 2
    qt = q_ref[qi * blk:(qi + 1) * blk, :].astype(F32).T
    feature = lax.broadcasted_iota(jnp.int32, (E, blk), 0)
    qst = jnp.concatenate([jnp.where(feature < half, qt, 0.0), jnp.where(feature >= half, qt, 0.0)],
                          axis=1).astype(BF16)
    key = lax.broadcasted_iota(jnp.int32, (blk, 2 * blk), 0)
    query = lax.broadcasted_iota(jnp.int32, (blk, 2 * blk), 1)
    causal = key <= jnp.where(query >= blk, query - blk, query)

    m = jnp.full((1, 2 * blk), MASK_VALUE, F32)
    l = jnp.zeros((1, 2 * blk), F32)
    acc = jnp.zeros((E, 2 * blk), F32)
    for c in range(qi + 1):
        s = jnp.dot(k_ref[c * blk:(c + 1) * blk, :], qst, preferred_element_type=F32)
        if c == qi:
            s = jnp.where(causal, s, MASK_VALUE)
        m_new = jnp.maximum(m, jnp.max(s, axis=0, keepdims=True))
        alpha = jnp.exp2(m - m_new)
        p = jnp.exp2(s - m_new)
        l = alpha * l + jnp.sum(p, axis=0, keepdims=True)
        acc = alpha * acc + jnp.dot(vt_scr[:, c * blk:(c + 1) * blk], p.astype(BF16), preferred_element_type=F32)
        m = m_new
    ot = acc[:, 0:blk] / l[:, 0:blk] - lam * (acc[:, blk:2 * blk] / l[:, blk:2 * blk])
    ot = ot * lax.rsqrt(jnp.mean(ot * ot, axis=0, keepdims=True) + SUBLN_EPS)
    ot = (ot * g_col_ref[...]) * (1.0 - lam_init)
    o_ref[qi * blk:(qi + 1) * blk, :] = ot.T.astype(o_ref.dtype)


def _sample_start(q_ref, kn_ref, vn_ref, qall_scr, bias_scr, m_scr, l_scr, acc_scr):
    S, H, E = q_ref.shape
    half = E // 2
    rr = lax.broadcasted_iota(jnp.int32, bias_scr.shape, 0)
    cc = lax.broadcasted_iota(jnp.int32, bias_scr.shape, 1)
    bias_scr[...] = jnp.where(cc % H == rr % H, 0.0, MASK_VALUE)
    lane = lax.broadcasted_iota(jnp.int32, (H, E), 1)
    for m in range(2):
        for s in range(S):
            rows = pl.ds((m * S + s) * H, H)
            q = jnp.where(lane // half == m, q_ref[s], 0.0)
            qall_scr[rows, :] = q.astype(BF16)
            sc = [jnp.sum(q * kn_ref[j], axis=1, keepdims=True) for j in range(s + 1)]
            m0 = functools.reduce(jnp.maximum, sc)
            ps = [jnp.exp2(c - m0) for c in sc]
            m_scr[rows, :] = m0
            l_scr[rows, :] = functools.reduce(jnp.add, ps)
            acc_scr[rows, :] = functools.reduce(jnp.add, [p * vn_ref[j] for j, p in enumerate(ps)])


def _sample_pages(k_refs, v_refs, *state):
    for first in range(0, len(k_refs), SAMPLE_PAGE_GROUP):
        group = slice(first, first + SAMPLE_PAGE_GROUP)
        _sample_page_group(k_refs[group], v_refs[group], *state)


def _sample_page_group(k_refs, v_refs, qall_scr, bias_scr, m_scr, l_scr, acc_scr):
    qall = qall_scr[...]
    bias = bias_scr[...]
    scores = []
    for k_ref in k_refs:
        kj = k_ref[...].astype(BF16)
        scores.append(lax.dot_general(qall, kj, NT_DIMS, preferred_element_type=F32) + bias)
    m_old = m_scr[...]
    m_new = m_old
    for s in scores:
        m_new = jnp.maximum(m_new, jnp.max(s, axis=1, keepdims=True))
    alpha = jnp.exp2(m_old - m_new)
    l_new = alpha * l_scr[...]
    acc = alpha * acc_scr[...]
    for s, v_ref in zip(scores, v_refs):
        p = jnp.exp2(s - m_new)
        l_new = l_new + jnp.sum(p, axis=1, keepdims=True)
        acc = acc + jnp.dot(p.astype(BF16), v_ref[...].astype(BF16), preferred_element_type=F32)
    m_scr[...] = m_new
    l_scr[...] = l_new
    acc_scr[...] = acc


def _attention_kernel(pt_ref, lamv_ref, g_ref, g_col_ref, q_ref, k_ref, v_ref, sq_ref, kn_ref, vn_ref, *refs,
                      blk, pages_per_step, steps_per_seq, lam_init):
    P = pages_per_step
    kp_refs, vp_refs = refs[:P], refs[P:2 * P]
    o_ref, so_ref, vt_scr, qall_scr, bias_scr, m_scr, l_scr, acc_scr = refs[2 * P:]
    sample_state = (qall_scr, bias_scr, m_scr, l_scr, acc_scr)
    S, H, E = sq_ref.shape
    n_blocks = q_ref.shape[0] // blk
    n_sub = n_blocks // 2
    step = pl.program_id(0)
    lam = _lambda_value(lamv_ref, lam_init)

    @pl.when(step % steps_per_seq == 0)
    def _():
        _sample_start(sq_ref, kn_ref, vn_ref, *sample_state)

    for j in range(n_sub):
        @pl.when(step % n_sub == j)
        def _(j=j):
            if j == 0:
                vt_scr[...] = v_ref[...].astype(F32).T.astype(BF16)
            _sample_pages(kp_refs, vp_refs, *sample_state)
            for qi in (j, n_blocks - 1 - j):
                _prompt_query_block(qi, lam, g_col_ref, q_ref, k_ref, vt_scr, o_ref, blk=blk, lam_init=lam_init)

    @pl.when(step % steps_per_seq == steps_per_seq - 1)
    def _():
        o = _head_finish(acc_scr[...], l_scr[...], lam, g_ref[...], 1.0 - lam_init, S * H)
        so_ref[...] = o.reshape(S, H, E)


def _attention(qkv, sq, k_new, v_new, cache_k, cache_v, page_table, lamv, g, *, layer, batch, blk, lam_init):
    M = qkv.shape[0]
    DB, S, H, E = sq.shape
    T = M // batch
    page_rows = cache_k.shape[2]
    n_pages = page_table.shape[1]
    n_sub = T // blk // 2
    n_steps = batch * H * n_sub
    assert T % (2 * blk) == 0 and (DB * n_pages) % n_steps == 0, "the page stream must split evenly over the steps"
    P = DB * n_pages // n_steps
    assert n_pages % P == 0
    steps_per_seq = n_pages // P
    R = 2 * S * H
    head = lambda offset: pl.BlockSpec((T, E), lambda i, pt: (i // (H * n_sub), offset + (i // n_sub) % H))
    per_seq = pl.BlockSpec((None, S, H, E), lambda i, pt: (i // steps_per_seq, 0, 0, 0))

    def page_spec(j):
        return pl.BlockSpec((None, None, page_rows, E),
                            lambda i, pt: (layer, pt[i // steps_per_seq, (i % steps_per_seq) * P + j], 0, 0))

    block_bytes = 2 * (2 * P * _nbytes((page_rows, E), F32) + 4 * _nbytes((T, E), BF16)) \
        + _nbytes((E, T), BF16) + 3 * _nbytes((blk, 2 * blk), F32) \
        + (2 + SAMPLE_PAGE_GROUP) * _nbytes((R, page_rows), F32)
    return pl.pallas_call(
        functools.partial(_attention_kernel, blk=blk, pages_per_step=P, steps_per_seq=steps_per_seq,
                          lam_init=lam_init),
        out_shape=[jax.ShapeDtypeStruct((M, H * E), BF16), jax.ShapeDtypeStruct((DB, S, H, E), F32)],
        grid_spec=pltpu.PrefetchScalarGridSpec(
            num_scalar_prefetch=1,
            grid=(n_steps,),
            in_specs=[
                pl.BlockSpec(lamv.shape, lambda i, pt: (0, 0)),
                pl.BlockSpec((1, E), lambda i, pt: (0, 0)),
                pl.BlockSpec((E, 1), lambda i, pt: (0, 0)),
                head(0), head(H), head(2 * H),
                per_seq, per_seq, per_seq,
            ] + [page_spec(j) for j in range(P)] + [page_spec(j) for j in range(P)],
            out_specs=[head(0), per_seq],
            scratch_shapes=[
                pltpu.VMEM((E, T), BF16),
                pltpu.VMEM((R, E), BF16), pltpu.VMEM((R, page_rows), F32),
                pltpu.VMEM((R, 1), F32), pltpu.VMEM((R, 1), F32), pltpu.VMEM((R, E), F32),
            ],
        ),
        compiler_params=_compiler_params(("arbitrary",), block_bytes),
        name="attention",
    )(page_table, lamv, g.reshape(1, E), g.reshape(E, 1), qkv, qkv, qkv, sq, k_new, v_new,
      *([cache_k] * P), *([cache_v] * P))


def _residual_tile(lhs_refs, w_refs, r_ref, g_ref, x_ref, xg_ref, ssq_ref):
    bn = x_ref.shape[1]
    bw = min(bn, MXU_WIDTH)
    ssq = None
    for c in range(bn // bw):
        cols = slice(c * bw, (c + 1) * bw)
        x = r_ref[:, cols]
        for lhs_ref, w_ref in zip(lhs_refs, w_refs):
            x = x + jnp.dot(lhs_ref[...], w_ref[:, cols], preferred_element_type=F32)
        x_ref[:, cols] = x
        xg_ref[:, cols] = (x * g_ref[:, cols]).astype(BF16)
        part = _lane_fold(x * x, jnp.add)
        ssq = part if ssq is None else ssq + part
    ssq_ref[...] = ssq


def _out_proj_kernel(a_ref, y_ref, w_ref, r_ref, g_ref, x_ref, xg_ref, ssq_ref, w_scr):
    @pl.when(pl.program_id(0) == 0)
    def _():
        w_scr[...] = w_ref[...].astype(BF16)

    ka = a_ref.shape[1]
    _residual_tile((a_ref, y_ref), (w_scr.at[0:ka, :], w_scr.at[ka:, :]), r_ref, g_ref, x_ref, xg_ref, ssq_ref)


def _out_proj(a, y, w, res, g, *, layer, bm):
    M, ka = a.shape
    K = ka + y.shape[1]
    N = w.shape[2]
    rows = lambda width: pl.BlockSpec((bm, width), lambda m: (m, 0))
    block_bytes = 2 * (_nbytes((bm, K), BF16) + 2 * _nbytes((bm, N), F32) + _nbytes((bm, N), BF16)) \
        + _nbytes((K, N), F32) + _nbytes((K, N), BF16) + 4 * _nbytes((bm, MXU_WIDTH), F32)
    return pl.pallas_call(
        _out_proj_kernel,
        out_shape=[jax.ShapeDtypeStruct((M, N), F32), jax.ShapeDtypeStruct((M, N), BF16),
                   jax.ShapeDtypeStruct((M, V7X_LANES), F32)],
        grid=(M // bm,),
        in_specs=[
            rows(ka), rows(K - ka),
            pl.BlockSpec((None, K, N), lambda m: (layer, 0, 0), pipeline_mode=pl.Buffered(1)),
            rows(N),
            pl.BlockSpec((1, N), lambda m: (0, 0)),
        ],
        out_specs=[rows(N), rows(N), rows(V7X_LANES)],
        scratch_shapes=[pltpu.VMEM((K, N), BF16)],
        compiler_params=_compiler_params(("arbitrary",), block_bytes),
        name="out_proj",
    )(a, y, w, res, g.reshape(1, N))


def _swiglu_tile(xg_ref, ssq_ref, wg_scr, wu_scr, o_ref):
    r = _row_rsqrt(ssq_ref, xg_ref.shape[1])
    bn = o_ref.shape[1]
    bw = min(bn, MXU_WIDTH)
    for c in range(bn // bw):
        cols = slice(c * bw, (c + 1) * bw)
        gate = jnp.dot(xg_ref[...], wg_scr[:, cols], preferred_element_type=F32) * r
        up = jnp.dot(xg_ref[...], wu_scr[:, cols], preferred_element_type=F32) * r
        o_ref[:, cols] = ((gate / (1.0 + jnp.exp(-gate))) * up).astype(BF16)


def _gate_up_kernel(xg_ref, ssq_ref, xsg_ref, sssq_ref, wg_ref, wu_ref, wd_ref,
                    o_ref, os_ref, wdb_ref, wg_scr, wu_scr):
    @pl.when(pl.program_id(1) == 0)
    def _():
        wg_scr[...] = wg_ref[...].astype(BF16)
        wu_scr[...] = wu_ref[...].astype(BF16)
        _swiglu_tile(xsg_ref, sssq_ref, wg_scr, wu_scr, os_ref)

    wdb_ref[...] = wd_ref[...].astype(BF16)
    _swiglu_tile(xg_ref, ssq_ref, wg_scr, wu_scr, o_ref)


def _gate_up(xg, ssq, xsg, sssq, wg, wu, wd, *, layer, bm, bn):
    M, D = xg.shape
    Ms = xsg.shape[0]
    F, N = wd.shape[1:]
    n_m = M // bm
    slab = F // (F // bn * n_m)
    w_spec = pl.BlockSpec((None, D, bn), lambda n, m: (layer, 0, n))
    block_bytes = 2 * (_nbytes((bm, D), BF16) + 2 * _nbytes((D, bn), F32) + _nbytes((bm, bn), BF16)
                       + _nbytes((slab, N), F32) + _nbytes((slab, N), BF16) + _nbytes((Ms, D), BF16)) \
        + 2 * _nbytes((D, bn), BF16) + 6 * _nbytes((bm, MXU_WIDTH), F32)
    return pl.pallas_call(
        _gate_up_kernel,
        out_shape=[jax.ShapeDtypeStruct((M, F), BF16), jax.ShapeDtypeStruct((Ms, F), BF16),
                   jax.ShapeDtypeStruct((F, N), BF16)],
        grid=(F // bn, n_m),
        in_specs=[
            pl.BlockSpec((bm, D), lambda n, m: (m, 0)),
            pl.BlockSpec((bm, ssq.shape[1]), lambda n, m: (m, 0)),
            pl.BlockSpec((Ms, D), lambda n, m: (0, 0)),
            pl.BlockSpec((Ms, sssq.shape[1]), lambda n, m: (0, 0)),
            w_spec, w_spec,
            pl.BlockSpec((None, slab, N), lambda n, m: (layer, n * n_m + m, 0)),
        ],
        out_specs=[
            pl.BlockSpec((bm, bn), lambda n, m: (m, n)),
            pl.BlockSpec((Ms, bn), lambda n, m: (0, n)),
            pl.BlockSpec((slab, N), lambda n, m: (n * n_m + m, 0)),
        ],
        scratch_shapes=[pltpu.VMEM((D, bn), BF16)] * 2,
        compiler_params=_compiler_params(("arbitrary", "arbitrary"), block_bytes),
        name="gate_up",
    )(xg, ssq, xsg, sssq, wg, wu, wd)


def _down_kernel(a_ref, w_ref, r_ref, g_ref, x_ref, xg_ref, ssq_ref):
    _residual_tile((a_ref,), (w_ref,), r_ref, g_ref, x_ref, xg_ref, ssq_ref)


def _down(a, w, res, g, *, bm, bn):
    M, F = a.shape
    N = w.shape[1]
    tile = lambda width: pl.BlockSpec((bm, width), lambda m, n: (m, n))
    block_bytes = 2 * (_nbytes((bm, F), BF16) + _nbytes((F, bn), BF16) + 2 * _nbytes((bm, bn), F32)
                       + _nbytes((bm, bn), BF16)) + 4 * _nbytes((bm, MXU_WIDTH), F32)
    return pl.pallas_call(
        _down_kernel,
        out_shape=[jax.ShapeDtypeStruct((M, N), F32), jax.ShapeDtypeStruct((M, N), BF16),
                   jax.ShapeDtypeStruct((M, N // bn * V7X_LANES), F32)],
        grid=(M // bm, N // bn),
        in_specs=[
            pl.BlockSpec((bm, F), lambda m, n: (m, 0)),
            pl.BlockSpec((F, bn), lambda m, n: (0, n)),
            tile(bn),
            pl.BlockSpec((1, bn), lambda m, n: (0, n)),
        ],
        out_specs=[tile(bn), tile(bn), tile(V7X_LANES)],
        compiler_params=_compiler_params(("arbitrary", "arbitrary"), block_bytes),
        name="down_proj",
    )(a, w, res, g.reshape(1, N))


def _final_norm_kernel(x_ref, g_ref, o_ref):
    o_ref[...] = _rms_scale(x_ref[...], RMS_EPS) * g_ref[...]


def _final_norm(x, g, *, bm):
    M, D = x.shape
    rows = pl.BlockSpec((bm, D), lambda m: (m, 0))
    return pl.pallas_call(
        _final_norm_kernel,
        out_shape=jax.ShapeDtypeStruct((M, D), F32),
        grid=(M // bm,),
        in_specs=[rows, pl.BlockSpec((1, D), lambda m: (0, 0))],
        out_specs=rows,
        compiler_params=_compiler_params(("arbitrary",), 4 * _nbytes((bm, D), F32)),
        name="final_norm",
    )(x, g.reshape(1, D))


def _pick_block(size, preferred):
    b = min(size, preferred)
    while size % b:
        b //= 2
    return b


def kernel(x_prompt, x_sample, cache_k, cache_v, state_conv, page_table, attn_norm, w_in, conv_w,
           lambda_q1, lambda_k1, lambda_q2, lambda_k2, subln_g, w_out, ffn_norm, w_gate, w_up,
           w_down, final_norm):
    B, T, D = x_prompt.shape
    DB, S, _ = x_sample.shape
    depth, pool, page, H, E = cache_k.shape
    assert E == V7X_LANES and H == V7X_SUBLANES, "one (head, value) tile per position is assumed"
    head_dim = E // 2
    W = H * E
    C = conv_w.shape[2]
    past = page_table.shape[1] * page
    Mp, Ms = B * T, DB * S

    cos_p, sin_p = _rope_tables(np.arange(T), head_dim)
    cos_s, sin_s = _rope_tables(np.tile(past + np.arange(S), DB), head_dim)
    ck = cache_k.reshape(depth, pool, page * H, E)
    cv = cache_v.reshape(depth, pool, page * H, E)

    bm_attn = _pick_block(T, 1024)
    bm_conv = _pick_block(T, 1024)
    bm_out = _pick_block(T, 512)
    bm_ff = _pick_block(T, 1024)
    bm_down = _pick_block(T, 1024)
    blk = _pick_block(T, 512)
    bc = _pick_block(C, 512)
    bn = _pick_block(D, 512)
    bn_ff = _pick_block(w_gate.shape[2], 512)

    xp = x_prompt.reshape(Mp, D)
    xs = x_sample.reshape(Ms, D)
    xpg, pssq = _prep(xp, attn_norm[0], bm=bm_ff)
    xsg, sssq = _prep(xs, attn_norm[0], bm=Ms)
    kv_prompt = (jnp.zeros((depth, Mp * H, E), F32), jnp.zeros((depth, Mp * H, E), F32))
    cp_l, ks_l, vs_l, cs_l = [], [], [], []
    for l in range(depth):
        lam_init = 0.8 - 0.6 * math.exp(-0.3 * l)
        lamv = jnp.stack([lambda_q1[l], lambda_k1[l], lambda_q2[l], lambda_k2[l]])
        g_next = attn_norm[l + 1] if l + 1 < depth else final_norm

        qkv, k_all, v_all = _attn_proj(xpg, pssq, w_in, cos_p, sin_p, kv_prompt, layer=l, slab=l,
                                       n_slabs=depth, bm=bm_attn, n_heads=H)
        kv_prompt = (k_all, v_all)
        yc, tail = _conv_proj(xpg, pssq, w_in, conv_w, None, layer=l, bm=bm_conv, bc=bc, seq=T)
        cp_l.append(tail[:, V7X_SUBLANES - (CONV_K - 1):])

        sqkv, k_new, v_new = _attn_proj(xsg, sssq, w_in, cos_s, sin_s, None, layer=l, slab=0, n_slabs=1,
                                        bm=Ms, n_heads=H)
        k_new = k_new.reshape(DB, S, H, E)
        v_new = v_new.reshape(DB, S, H, E)
        ycs, u = _conv_proj(xsg, sssq, w_in, conv_w, _sample_conv_prev(state_conv[l], S),
                            layer=l, bm=Ms, bc=bc, seq=S)

        ao, aos = _attention(qkv, sqkv[:, :W].astype(F32).reshape(DB, S, H, E), k_new, v_new, ck, cv,
                             page_table, lamv, subln_g[l], layer=l, batch=B, blk=blk, lam_init=lam_init)
        xp, xpg, pssq = _out_proj(ao, yc, w_out, xp, ffn_norm[l], layer=l, bm=bm_out)
        xs, xsg, sssq = _out_proj(aos.reshape(Ms, W).astype(BF16), ycs, w_out, xs, ffn_norm[l], layer=l, bm=Ms)
        act_p, act_s, wd = _gate_up(xpg, pssq, xsg, sssq, w_gate, w_up, w_down, layer=l, bm=bm_ff, bn=bn_ff)
        xp, xpg, pssq = _down(act_p, wd, xp, g_next, bm=bm_down, bn=bn)
        xs, xsg, sssq = _down(act_s, wd, xs, g_next, bm=Ms, bn=bn)
        ks_l.append(k_new)
        vs_l.append(v_new)
        cs_l.append(u.reshape(DB, S, C)[:, S - (CONV_K - 1):])

    y_prompt = _final_norm(xp, final_norm, bm=bm_out).reshape(B, T, D)
    y_sample = _final_norm(xs, final_norm, bm=Ms).reshape(DB, S, D)
    k_prompt, v_prompt = (a.reshape(depth, B, T, H, E) for a in kv_prompt)
    return (y_prompt, y_sample, k_prompt, v_prompt, jnp.stack(cp_l),
            jnp.stack(ks_l), jnp.stack(vs_l), jnp.stack(cs_l))
```

```python
import functools
import math

import numpy as np
import jax
import jax.numpy as jnp
from jax import lax
from jax.experimental import pallas as pl
from jax.experimental.pallas import tpu as pltpu

ROPE_THETA = 10000.0
RMS_EPS = 1e-6
SUBLN_EPS = 1e-5
CONV_K = 3
MASK_VALUE = -1e30
SAMPLE_PAGE_GROUP = 8
LOG2_E = math.log2(math.e)

V7X_LANES = 128
V7X_SUBLANES = 8
V7X_VMEM_BYTES = 64 * 1024 * 1024
MXU_WIDTH = 256
VMEM_TEMP_BYTES = 12 * 1024 * 1024

F32 = jnp.float32
BF16 = jnp.bfloat16
NT_DIMS = (((1,), (1,)), ((), ()))


def _compiler_params(semantics, block_bytes):
    limit = min(block_bytes + VMEM_TEMP_BYTES, V7X_VMEM_BYTES - 6 * 1024 * 1024)
    return pltpu.CompilerParams(dimension_semantics=semantics, vmem_limit_bytes=int(limit))


def _weight_buffering(n_row_tiles):
    return pl.Buffered(1) if n_row_tiles > 1 else None


def _nbytes(shape, dtype):
    return int(np.prod(shape)) * jnp.dtype(dtype).itemsize


def _rms_scale(x, eps):
    return x * lax.rsqrt(jnp.mean(x * x, axis=-1, keepdims=True) + eps)


def _row_rsqrt(ssq_ref, width):
    return lax.rsqrt(jnp.sum(ssq_ref[...], axis=1, keepdims=True) * (1.0 / width) + RMS_EPS)


def _lane_fold(x, op):
    return functools.reduce(op, [x[:, j * V7X_LANES:(j + 1) * V7X_LANES] for j in range(x.shape[1] // V7X_LANES)])


def _emit_norm_inputs(x, g_ref, xg_ref, ssq_ref):
    xg_ref[...] = (x * g_ref[...]).astype(BF16)
    ssq_ref[...] = _lane_fold(x * x, jnp.add)


def _lambda_value(lamv_ref, lam_init):
    d1 = jnp.sum(lamv_ref[0:1, :] * lamv_ref[1:2, :], axis=1, keepdims=True)
    d2 = jnp.sum(lamv_ref[2:3, :] * lamv_ref[3:4, :], axis=1, keepdims=True)
    return jnp.exp(d1) - jnp.exp(d2) + lam_init


def _head_finish(acc, l, lam, g, out_scale, rows):
    o = acc[0:rows] / l[0:rows] - lam * (acc[rows:2 * rows] / l[rows:2 * rows])
    return (_rms_scale(o, SUBLN_EPS) * g) * out_scale


def _prep_kernel(x_ref, g_ref, xg_ref, ssq_ref):
    _emit_norm_inputs(x_ref[...], g_ref, xg_ref, ssq_ref)


def _prep(x, g, *, bm):
    M, D = x.shape
    rows = lambda width: pl.BlockSpec((bm, width), lambda m: (m, 0))
    return pl.pallas_call(
        _prep_kernel,
        out_shape=[jax.ShapeDtypeStruct((M, D), BF16), jax.ShapeDtypeStruct((M, V7X_LANES), F32)],
        grid=(M // bm,),
        in_specs=[rows(D), pl.BlockSpec((1, D), lambda m: (0, 0))],
        out_specs=[rows(D), rows(V7X_LANES)],
        compiler_params=_compiler_params(("arbitrary",), 3 * _nbytes((bm, D), F32)),
        name="prep",
    )(x, g.reshape(1, D))


def _rope_tables(positions, head_dim):
    inv = 1.0 / (ROPE_THETA ** (np.arange(0, head_dim, 2, dtype=np.float64) / head_dim))
    ang = np.asarray(positions, np.float64)[:, None] * inv[None, :]
    ang = np.concatenate([ang, ang], axis=-1)
    sign = np.concatenate([-np.ones(head_dim // 2), np.ones(head_dim // 2)])
    reps = V7X_LANES // head_dim
    cos = np.tile(np.cos(ang), (1, reps)).astype(np.float32)
    sin = np.tile(np.sin(ang) * sign[None, :], (1, reps)).astype(np.float32)
    return jnp.asarray(cos), jnp.asarray(sin)


def _rope(zc, cos, sin, low, half):
    ahead = pltpu.roll(zc, V7X_LANES - half, axis=1)
    behind = pltpu.roll(zc, half, axis=1)
    return zc * cos + jnp.where(low, ahead, behind) * sin


def _attn_proj_kernel(xg_ref, ssq_ref, w_ref, cos_ref, sin_ref, *refs, half, q_scale):
    qkv_ref, k_ref, v_ref, w_scr = refs[-4:]
    n = pl.program_id(0)
    bm, W = qkv_ref.shape
    E = k_ref.shape[1]
    H = W // E

    @pl.when(pl.program_id(1) == 0)
    def _():
        w_scr[...] = w_ref[...].astype(BF16)

    r = _row_rsqrt(ssq_ref, xg_ref.shape[1])
    lane = lax.broadcasted_iota(jnp.int32, (bm, E), 1)
    low = (lane % (2 * half)) < half

    def heads():
        per_group = MXU_WIDTH // E
        for c in range(H // per_group):
            cols = slice(c * MXU_WIDTH, (c + 1) * MXU_WIDTH)
            z = jnp.dot(xg_ref[...], w_scr[:, cols], preferred_element_type=F32) * r
            for j in range(per_group):
                yield c * per_group + j, z[:, j * E:(j + 1) * E]

    @pl.when(n == 0)
    def _():
        for h, z in heads():
            q = _rope(z, cos_ref[...], sin_ref[...], low, half)
            qkv_ref[:, h * E:(h + 1) * E] = (q * q_scale).astype(BF16)

    @pl.when(n == 1)
    def _():
        for h, z in heads():
            k = _rope(z, cos_ref[...], sin_ref[...], low, half)
            qkv_ref[:, h * E:(h + 1) * E] = k.astype(BF16)
            k_ref[pl.ds(h, bm, stride=H), :] = k

    @pl.when(n == 2)
    def _():
        for h, z in heads():
            qkv_ref[:, h * E:(h + 1) * E] = z.astype(BF16)
            v_ref[pl.ds(h, bm, stride=H), :] = z


def _attn_proj(xg, ssq, w_in, cos, sin, kv_all, *, layer, slab, n_slabs, bm, n_heads):
    M, D = xg.shape
    W = w_in.shape[2] // 6
    H = n_heads
    E = W // H
    head_dim = E // 2
    n_row_tiles = M // bm
    n_pos_blocks = cos.shape[0] // bm
    k_rows = lambda n, m: (slab, jnp.where(n == 1, m, jnp.where(n < 1, 0, n_row_tiles - 1)), 0)
    v_rows = lambda n, m: (slab, jnp.where(n == 2, m, 0), 0)
    stacked = jax.ShapeDtypeStruct((n_slabs, M * H, E), F32)
    aliases = {} if kv_all is None else {5: 1, 6: 2}
    extra_in = [] if kv_all is None else [pl.BlockSpec(memory_space=pl.ANY)] * 2
    block_bytes = 2 * (_nbytes((bm, D), BF16) + _nbytes((bm, W), BF16) + 2 * _nbytes((bm, W), F32)) \
        + _nbytes((D, W), F32) + _nbytes((D, W), BF16) + 4 * _nbytes((bm, MXU_WIDTH), F32)
    return pl.pallas_call(
        functools.partial(_attn_proj_kernel, half=head_dim // 2, q_scale=head_dim ** -0.5 * LOG2_E),
        out_shape=[jax.ShapeDtypeStruct((M, 3 * W), BF16), stacked, stacked],
        grid=(3, n_row_tiles),
        in_specs=[
            pl.BlockSpec((bm, D), lambda n, m: (m, 0)),
            pl.BlockSpec((bm, ssq.shape[1]), lambda n, m: (m, 0)),
            pl.BlockSpec((None, D, W), lambda n, m: (layer, 0, n), pipeline_mode=_weight_buffering(n_row_tiles)),
            pl.BlockSpec((bm, V7X_LANES), lambda n, m: (m % n_pos_blocks, 0)),
            pl.BlockSpec((bm, V7X_LANES), lambda n, m: (m % n_pos_blocks, 0)),
        ] + extra_in,
        out_specs=[
            pl.BlockSpec((bm, W), lambda n, m: (m, n)),
            pl.BlockSpec((None, bm * H, E), k_rows),
            pl.BlockSpec((None, bm * H, E), v_rows),
        ],
        scratch_shapes=[pltpu.VMEM((D, W), BF16)],
        input_output_aliases=aliases,
        compiler_params=_compiler_params(("arbitrary", "arbitrary"), block_bytes),
        name="attn_proj",
    )(xg, ssq, w_in, cos, sin, *([] if kv_all is None else kv_all))


def _conv_proj_kernel(xg_ref, ssq_ref, wb_ref, wc_ref, wh_ref, cw_ref, *refs, seq, tiles_per_seq):
    carried = tiles_per_seq is not None
    if carried:
        y_ref, tail_ref, wb_scr, wc_scr, wh_scr, carry_scr = refs
    else:
        prev1_ref, prev2_ref, y_ref, tail_ref, wb_scr, wc_scr, wh_scr = refs
    m = pl.program_id(1)
    bm, bc = y_ref.shape

    @pl.when(m == 0)
    def _():
        wb_scr[...] = wb_ref[...].astype(BF16)
        wc_scr[...] = wc_ref[...].astype(BF16)
        wh_scr[...] = wh_ref[...].astype(BF16)

    r = _row_rsqrt(ssq_ref, xg_ref.shape[1])
    if carried:
        @pl.when(m % tiles_per_seq == 0)
        def _():
            carry_scr[...] = jnp.zeros(carry_scr.shape, F32)

    bw = min(bc, MXU_WIDTH)
    t = lax.broadcasted_iota(jnp.int32, (bm, bw), 0)
    for c in range(bc // bw):
        cols = slice(c * bw, (c + 1) * bw)
        gb = jnp.dot(xg_ref[...], wb_scr[:, cols], preferred_element_type=F32) * r
        gc = jnp.dot(xg_ref[...], wc_scr[:, cols], preferred_element_type=F32) * r
        hc = jnp.dot(xg_ref[...], wh_scr[:, cols], preferred_element_type=F32) * r
        u = gc * hc
        if carried:
            last = jnp.broadcast_to(carry_scr[V7X_SUBLANES - 1:V7X_SUBLANES, cols], (bm, bw))
            second_last = jnp.broadcast_to(carry_scr[V7X_SUBLANES - 2:V7X_SUBLANES - 1, cols], (bm, bw))
            prev1 = last
            prev2 = jnp.where(t == 0, second_last, last)
            pos = t
        else:
            prev1 = prev1_ref[:, cols]
            prev2 = prev2_ref[:, cols]
            pos = t % seq
        sh1 = jnp.where(pos >= 1, pltpu.roll(u, 1, axis=0), prev1)
        sh2 = jnp.where(pos >= 2, pltpu.roll(u, 2, axis=0), prev2)
        y = gb * (cw_ref[0:1, cols] * sh2 + cw_ref[1:2, cols] * sh1 + cw_ref[2:3, cols] * u)
        y_ref[:, cols] = y.astype(BF16)
        if carried:
            tail = u[bm - V7X_SUBLANES:bm, :]
            carry_scr[:, cols] = tail
            tail_ref[:, cols] = tail
        else:
            tail_ref[:, cols] = u


def _conv_proj(xg, ssq, w_in, conv_w, prev, *, layer, bm, bc, seq):
    M, D = xg.shape
    C = conv_w.shape[2]
    col0 = (w_in.shape[2] - 3 * C) // bc
    w_spec = lambda j: pl.BlockSpec((None, D, bc), lambda c, m: (layer, 0, col0 + j * (C // bc) + c),
                                    pipeline_mode=_weight_buffering(M // bm))
    tile = pl.BlockSpec((bm, bc), lambda c, m: (m, c))
    in_specs = [
        pl.BlockSpec((bm, D), lambda c, m: (m, 0)),
        pl.BlockSpec((bm, ssq.shape[1]), lambda c, m: (m, 0)),
        w_spec(0), w_spec(1), w_spec(2),
        pl.BlockSpec((None, CONV_K, bc), lambda c, m: (layer, 0, c)),
    ]
    scratch = [pltpu.VMEM((D, bc), BF16)] * 3
    if prev is None:
        tiles_per_seq = seq // bm
        batch = M // seq
        tail_shape = jax.ShapeDtypeStruct((batch, V7X_SUBLANES, C), F32)
        tail_spec = pl.BlockSpec((None, V7X_SUBLANES, bc), lambda c, m: (m // tiles_per_seq, 0, c))
        scratch = scratch + [pltpu.VMEM((V7X_SUBLANES, bc), F32)]
        operands = ()
    else:
        tiles_per_seq = None
        tail_shape = jax.ShapeDtypeStruct((M, C), F32)
        tail_spec = tile
        in_specs = in_specs + [tile, tile]
        operands = prev
    block_bytes = 2 * (_nbytes((bm, D), BF16) + 4 * _nbytes((bm, bc), F32)) \
        + 3 * _nbytes((D, bc), F32) + 3 * _nbytes((D, bc), BF16) + 8 * _nbytes((bm, MXU_WIDTH), F32)
    return pl.pallas_call(
        functools.partial(_conv_proj_kernel, seq=seq, tiles_per_seq=tiles_per_seq),
        out_shape=[jax.ShapeDtypeStruct((M, C), BF16), tail_shape],
        grid=(C // bc, M // bm),
        in_specs=in_specs,
        out_specs=[tile, tail_spec],
        scratch_shapes=scratch,
        compiler_params=_compiler_params(("arbitrary", "arbitrary"), block_bytes),
        name="conv_proj",
    )(xg, ssq, w_in, w_in, w_in, conv_w, *operands)


def _sample_conv_prev(state, seq):
    DB, _, C = state.shape
    zeros = jnp.zeros((DB, seq - 1, C), F32)
    prev1 = jnp.concatenate([state[:, 1:2], zeros], axis=1).reshape(DB * seq, C)
    prev2 = jnp.concatenate([state, zeros[:, 1:]], axis=1).reshape(DB * seq, C)
    return prev1, prev2


def _prompt_query_block(qi, lam, g_col_ref, q_ref, k_ref, vt_scr, o_ref, *, blk, lam_init):
    E = q_ref.shape[1]
    half = E // 2
    qt = q_ref[qi * blk:(qi + 1) * blk, :].astype(F32).T
    feature = lax.broadcasted_iota(jnp.int32, (E, blk), 0)
    qst = jnp.concatenate([jnp.where(feature < half, qt, 0.0), jnp.where(feature >= half, qt, 0.0)],
                          axis=1).astype(BF16)
    key = lax.broadcasted_iota(jnp.int32, (blk, 2 * blk), 0)
    query = lax.broadcasted_iota(jnp.int32, (blk, 2 * blk), 1)
    causal = key <= jnp.where(query >= blk, query - blk, query)

    m = jnp.full((1, 2 * blk), MASK_VALUE, F32)
    l = jnp.zeros((1, 2 * blk), F32)
    acc = jnp.zeros((E, 2 * blk), F32)
    for c in range(qi + 1):
        s = jnp.dot(k_ref[c * blk:(c + 1) * blk, :], qst, preferred_element_type=F32)
        if c == qi:
            s = jnp.where(causal, s, MASK_VALUE)
        m_new = jnp.maximum(m, jnp.max(s, axis=0, keepdims=True))
        alpha = jnp.exp2(m - m_new)
        p = jnp.exp2(s - m_new)
        l = alpha * l + jnp.sum(p, axis=0, keepdims=True)
        acc = alpha * acc + jnp.dot(vt_scr[:, c * blk:(c + 1) * blk], p.astype(BF16), preferred_element_type=F32)
        m = m_new
    ot = acc[:, 0:blk] / l[:, 0:blk] - lam * (acc[:, blk:2 * blk] / l[:, blk:2 * blk])
    ot = ot * lax.rsqrt(jnp.mean(ot * ot, axis=0, keepdims=True) + SUBLN_EPS)
    ot = (ot * g_col_ref[...]) * (1.0 - lam_init)
    o_ref[qi * blk:(qi + 1) * blk, :] = ot.T.astype(o_ref.dtype)


def _sample_start(q_ref, kn_ref, vn_ref, qall_scr, bias_scr, m_scr, l_scr, acc_scr):
    S, H, E = q_ref.shape
    half = E // 2
    rr = lax.broadcasted_iota(jnp.int32, bias_scr.shape, 0)
    cc = lax.broadcasted_iota(jnp.int32, bias_scr.shape, 1)
    bias_scr[...] = jnp.where(cc % H == rr % H, 0.0, MASK_VALUE)
    lane = lax.broadcasted_iota(jnp.int32, (H, E), 1)
    for m in range(2):
        for s in range(S):
            rows = pl.ds((m * S + s) * H, H)
            q = jnp.where(lane // half == m, q_ref[s], 0.0)
            qall_scr[rows, :] = q.astype(BF16)
            sc = [jnp.sum(q * kn_ref[j], axis=1, keepdims=True) for j in range(s + 1)]
            m0 = functools.reduce(jnp.maximum, sc)
            ps = [jnp.exp2(c - m0) for c in sc]
            m_scr[rows, :] = m0
            l_scr[rows, :] = functools.reduce(jnp.add, ps)
            acc_scr[rows, :] = functools.reduce(jnp.add, [p * vn_ref[j] for j, p in enumerate(ps)])


def _sample_pages(k_refs, v_refs, *state):
    for first in range(0, len(k_refs), SAMPLE_PAGE_GROUP):
        group = slice(first, first + SAMPLE_PAGE_GROUP)
        _sample_page_group(k_refs[group], v_refs[group], *state)


def _sample_page_group(k_refs, v_refs, qall_scr, bias_scr, m_scr, l_scr, acc_scr):
    qall = qall_scr[...]
    bias = bias_scr[...]
    scores = []
    for k_ref in k_refs:
        kj = k_ref[...].astype(BF16)
        scores.append(lax.dot_general(qall, kj, NT_DIMS, preferred_element_type=F32) + bias)
    m_old = m_scr[...]
    m_new = m_old
    for s in scores:
        m_new = jnp.maximum(m_new, jnp.max(s, axis=1, keepdims=True))
    alpha = jnp.exp2(m_old - m_new)
    l_new = alpha * l_scr[...]
    acc = alpha * acc_scr[...]
    for s, v_ref in zip(scores, v_refs):
        p = jnp.exp2(s - m_new)
        l_new = l_new + jnp.sum(p, axis=1, keepdims=True)
        acc = acc + jnp.dot(p.astype(BF16), v_ref[...].astype(BF16), preferred_element_type=F32)
    m_scr[...] = m_new
    l_scr[...] = l_new
    acc_scr[...] = acc


def _attention_kernel(pt_ref, lamv_ref, g_ref, g_col_ref, q_ref, k_ref, v_ref, sq_ref, kn_ref, vn_ref, *refs,
                      blk, pages_per_step, steps_per_seq, lam_init):
    P = pages_per_step
    kp_refs, vp_refs = refs[:P], refs[P:2 * P]
    o_ref, so_ref, vt_scr, qall_scr, bias_scr, m_scr, l_scr, acc_scr = refs[2 * P:]
    sample_state = (qall_scr, bias_scr, m_scr, l_scr, acc_scr)
    S, H, E = sq_ref.shape
    n_blocks = q_ref.shape[0] // blk
    n_sub = n_blocks // 2
    step = pl.program_id(0)
    lam = _lambda_value(lamv_ref, lam_init)

    @pl.when(step % steps_per_seq == 0)
    def _():
        _sample_start(sq_ref, kn_ref, vn_ref, *sample_state)

    for j in range(n_sub):
        @pl.when(step % n_sub == j)
        def _(j=j):
            if j == 0:
                vt_scr[...] = v_ref[...].astype(F32).T.astype(BF16)
            _sample_pages(kp_refs, vp_refs, *sample_state)
            for qi in (j, n_blocks - 1 - j):
                _prompt_query_block(qi, lam, g_col_ref, q_ref, k_ref, vt_scr, o_ref, blk=blk, lam_init=lam_init)

    @pl.when(step % steps_per_seq == steps_per_seq - 1)
    def _():
        o = _head_finish(acc_scr[...], l_scr[...], lam, g_ref[...], 1.0 - lam_init, S * H)
        so_ref[...] = o.reshape(S, H, E)


def _attention(qkv, sq, k_new, v_new, cache_k, cache_v, page_table, lamv, g, *, layer, batch, blk, lam_init):
    M = qkv.shape[0]
    DB, S, H, E = sq.shape
    T = M // batch
    page_rows = cache_k.shape[2]
    n_pages = page_table.shape[1]
    n_sub = T // blk // 2
    n_steps = batch * H * n_sub
    assert T % (2 * blk) == 0 and (DB * n_pages) % n_steps == 0, "the page stream must split evenly over the steps"
    P = DB * n_pages // n_steps
    assert n_pages % P == 0
    steps_per_seq = n_pages // P
    R = 2 * S * H
    head = lambda offset: pl.BlockSpec((T, E), lambda i, pt: (i // (H * n_sub), offset + (i // n_sub) % H))
    per_seq = pl.BlockSpec((None, S, H, E), lambda i, pt: (i // steps_per_seq, 0, 0, 0))

    def page_spec(j):
        return pl.BlockSpec((None, None, page_rows, E),
                            lambda i, pt: (layer, pt[i // steps_per_seq, (i % steps_per_seq) * P + j], 0, 0))

    block_bytes = 2 * (2 * P * _nbytes((page_rows, E), F32) + 4 * _nbytes((T, E), BF16)) \
        + _nbytes((E, T), BF16) + 3 * _nbytes((blk, 2 * blk), F32) \
        + (2 + SAMPLE_PAGE_GROUP) * _nbytes((R, page_rows), F32)
    return pl.pallas_call(
        functools.partial(_attention_kernel, blk=blk, pages_per_step=P, steps_per_seq=steps_per_seq,
                          lam_init=lam_init),
        out_shape=[jax.ShapeDtypeStruct((M, H * E), BF16), jax.ShapeDtypeStruct((DB, S, H, E), F32)],
        grid_spec=pltpu.PrefetchScalarGridSpec(
            num_scalar_prefetch=1,
            grid=(n_steps,),
            in_specs=[
                pl.BlockSpec(lamv.shape, lambda i, pt: (0, 0)),
                pl.BlockSpec((1, E), lambda i, pt: (0, 0)),
                pl.BlockSpec((E, 1), lambda i, pt: (0, 0)),
                head(0), head(H), head(2 * H),
                per_seq, per_seq, per_seq,
            ] + [page_spec(j) for j in range(P)] + [page_spec(j) for j in range(P)],
            out_specs=[head(0), per_seq],
            scratch_shapes=[
                pltpu.VMEM((E, T), BF16),
                pltpu.VMEM((R, E), BF16), pltpu.VMEM((R, page_rows), F32),
                pltpu.VMEM((R, 1), F32), pltpu.VMEM((R, 1), F32), pltpu.VMEM((R, E), F32),
            ],
        ),
        compiler_params=_compiler_params(("arbitrary",), block_bytes),
        name="attention",
    )(page_table, lamv, g.reshape(1, E), g.reshape(E, 1), qkv, qkv, qkv, sq, k_new, v_new,
      *([cache_k] * P), *([cache_v] * P))


def _residual_tile(lhs_refs, w_refs, r_ref, g_ref, x_ref, xg_ref, ssq_ref):
    bn = x_ref.shape[1]
    bw = min(bn, MXU_WIDTH)
    ssq = None
    for c in range(bn // bw):
        cols = slice(c * bw, (c + 1) * bw)
        x = r_ref[:, cols]
        for lhs_ref, w_ref in zip(lhs_refs, w_refs):
            x = x + jnp.dot(lhs_ref[...], w_ref[:, cols], preferred_element_type=F32)
        x_ref[:, cols] = x
        xg_ref[:, cols] = (x * g_ref[:, cols]).astype(BF16)
        part = _lane_fold(x * x, jnp.add)
        ssq = part if ssq is None else ssq + part
    ssq_ref[...] = ssq


def _out_proj_kernel(a_ref, y_ref, r_ref, as_ref, ys_ref, rs_ref, w_ref, g_ref,
                     x_ref, xg_ref, ssq_ref, xs_ref, xsg_ref, sssq_ref, w_scr):
    ka = a_ref.shape[1]
    weights = (w_scr.at[0:ka, :], w_scr.at[ka:, :])

    @pl.when(pl.program_id(0) == 0)
    def _():
        w_scr[...] = w_ref[...].astype(BF16)
        _residual_tile((as_ref, ys_ref), weights, rs_ref, g_ref, xs_ref, xsg_ref, sssq_ref)

    _residual_tile((a_ref, y_ref), weights, r_ref, g_ref, x_ref, xg_ref, ssq_ref)


def _out_proj(a, y, res, a_s, y_s, res_s, w, g, *, layer, bm):
    M, ka = a.shape
    Ms = a_s.shape[0]
    K = ka + y.shape[1]
    N = w.shape[2]
    rows = lambda width: pl.BlockSpec((bm, width), lambda m: (m, 0))
    whole = lambda width: pl.BlockSpec((Ms, width), lambda m: (0, 0))
    block_bytes = 2 * (_nbytes((bm + Ms, K), BF16) + 2 * _nbytes((bm + Ms, N), F32) + _nbytes((bm + Ms, N), BF16)) \
        + _nbytes((K, N), F32) + _nbytes((K, N), BF16) + 4 * _nbytes((bm, MXU_WIDTH), F32)
    out = lambda rows_: [jax.ShapeDtypeStruct((rows_, N), F32), jax.ShapeDtypeStruct((rows_, N), BF16),
                         jax.ShapeDtypeStruct((rows_, V7X_LANES), F32)]
    return pl.pallas_call(
        _out_proj_kernel,
        out_shape=out(M) + out(Ms),
        grid=(M // bm,),
        in_specs=[
            rows(ka), rows(K - ka), rows(N),
            whole(ka), whole(K - ka), whole(N),
            pl.BlockSpec((None, K, N), lambda m: (layer, 0, 0), pipeline_mode=pl.Buffered(1)),
            pl.BlockSpec((1, N), lambda m: (0, 0)),
        ],
        out_specs=[rows(N), rows(N), rows(V7X_LANES), whole(N), whole(N), whole(V7X_LANES)],
        scratch_shapes=[pltpu.VMEM((K, N), BF16)],
        compiler_params=_compiler_params(("arbitrary",), block_bytes),
        name="out_proj",
    )(a, y, res, a_s, y_s, res_s, w, g.reshape(1, N))


def _swiglu_tile(xg_ref, ssq_ref, wg_scr, wu_scr, o_ref):
    r = _row_rsqrt(ssq_ref, xg_ref.shape[1])
    bn = o_ref.shape[1]
    bw = min(bn, MXU_WIDTH)
    for c in range(bn // bw):
        cols = slice(c * bw, (c + 1) * bw)
        gate = jnp.dot(xg_ref[...], wg_scr[:, cols], preferred_element_type=F32) * r
        up = jnp.dot(xg_ref[...], wu_scr[:, cols], preferred_element_type=F32) * r
        o_ref[:, cols] = ((gate / (1.0 + jnp.exp(-gate))) * up).astype(BF16)


def _gate_up_kernel(xg_ref, ssq_ref, xsg_ref, sssq_ref, wg_ref, wu_ref, wd_ref,
                    o_ref, os_ref, wdb_ref, wg_scr, wu_scr):
    @pl.when(pl.program_id(1) == 0)
    def _():
        wg_scr[...] = wg_ref[...].astype(BF16)
        wu_scr[...] = wu_ref[...].astype(BF16)
        _swiglu_tile(xsg_ref, sssq_ref, wg_scr, wu_scr, os_ref)

    wdb_ref[...] = wd_ref[...].astype(BF16)
    _swiglu_tile(xg_ref, ssq_ref, wg_scr, wu_scr, o_ref)


def _gate_up(xg, ssq, xsg, sssq, wg, wu, wd, *, layer, bm, bn):
    M, D = xg.shape
    Ms = xsg.shape[0]
    F, N = wd.shape[1:]
    n_m = M // bm
    slab = F // (F // bn * n_m)
    w_spec = pl.BlockSpec((None, D, bn), lambda n, m: (layer, 0, n))
    block_bytes = 2 * (_nbytes((bm, D), BF16) + 2 * _nbytes((D, bn), F32) + _nbytes((bm, bn), BF16)
                       + _nbytes((slab, N), F32) + _nbytes((slab, N), BF16) + _nbytes((Ms, D), BF16)) \
        + 2 * _nbytes((D, bn), BF16) + 6 * _nbytes((bm, MXU_WIDTH), F32)
    return pl.pallas_call(
        _gate_up_kernel,
        out_shape=[jax.ShapeDtypeStruct((M, F), BF16), jax.ShapeDtypeStruct((Ms, F), BF16),
                   jax.ShapeDtypeStruct((F, N), BF16)],
        grid=(F // bn, n_m),
        in_specs=[
            pl.BlockSpec((bm, D), lambda n, m: (m, 0)),
            pl.BlockSpec((bm, ssq.shape[1]), lambda n, m: (m, 0)),
            pl.BlockSpec((Ms, D), lambda n, m: (0, 0)),
            pl.BlockSpec((Ms, sssq.shape[1]), lambda n, m: (0, 0)),
            w_spec, w_spec,
            pl.BlockSpec((None, slab, N), lambda n, m: (layer, n * n_m + m, 0)),
        ],
        out_specs=[
            pl.BlockSpec((bm, bn), lambda n, m: (m, n)),
            pl.BlockSpec((Ms, bn), lambda n, m: (0, n)),
            pl.BlockSpec((slab, N), lambda n, m: (n * n_m + m, 0)),
        ],
        scratch_shapes=[pltpu.VMEM((D, bn), BF16)] * 2,
        compiler_params=_compiler_params(("arbitrary", "arbitrary"), block_bytes),
        name="gate_up",
    )(xg, ssq, xsg, sssq, wg, wu, wd)


def _down_kernel(a_ref, w_ref, r_ref, g_ref, x_ref, xg_ref, ssq_ref):
    _residual_tile((a_ref,), (w_ref,), r_ref, g_ref, x_ref, xg_ref, ssq_ref)


def _down(a, w, res, g, *, bm, bn):
    M, F = a.shape
    N = w.shape[1]
    tile = lambda width: pl.BlockSpec((bm, width), lambda m, n: (m, n))
    block_bytes = 2 * (_nbytes((bm, F), BF16) + _nbytes((F, bn), BF16) + 2 * _nbytes((bm, bn), F32)
                       + _nbytes((bm, bn), BF16)) + 4 * _nbytes((bm, MXU_WIDTH), F32)
    return pl.pallas_call(
        _down_kernel,
        out_shape=[jax.ShapeDtypeStruct((M, N), F32), jax.ShapeDtypeStruct((M, N), BF16),
                   jax.ShapeDtypeStruct((M, N // bn * V7X_LANES), F32)],
        grid=(M // bm, N // bn),
        in_specs=[
            pl.BlockSpec((bm, F), lambda m, n: (m, 0)),
            pl.BlockSpec((F, bn), lambda m, n: (0, n)),
            tile(bn),
            pl.BlockSpec((1, bn), lambda m, n: (0, n)),
        ],
        out_specs=[tile(bn), tile(bn), tile(V7X_LANES)],
        compiler_params=_compiler_params(("arbitrary", "arbitrary"), block_bytes),
        name="down_proj",
    )(a, w, res, g.reshape(1, N))


def _final_norm_kernel(x_ref, g_ref, o_ref):
    o_ref[...] = _rms_scale(x_ref[...], RMS_EPS) * g_ref[...]


def _final_norm(x, g, *, bm):
    M, D = x.shape
    rows = pl.BlockSpec((bm, D), lambda m: (m, 0))
    return pl.pallas_call(
        _final_norm_kernel,
        out_shape=jax.ShapeDtypeStruct((M, D), F32),
        grid=(M // bm,),
        in_specs=[rows, pl.BlockSpec((1, D), lambda m: (0, 0))],
        out_specs=rows,
        compiler_params=_compiler_params(("arbitrary",), 4 * _nbytes((bm, D), F32)),
        name="final_norm",
    )(x, g.reshape(1, D))


def _pick_block(size, preferred):
    b = min(size, preferred)
    while size % b:
        b //= 2
    return b


def kernel(x_prompt, x_sample, cache_k, cache_v, state_conv, page_table, attn_norm, w_in, conv_w,
           lambda_q1, lambda_k1, lambda_q2, lambda_k2, subln_g, w_out, ffn_norm, w_gate, w_up,
           w_down, final_norm):
    B, T, D = x_prompt.shape
    DB, S, _ = x_sample.shape
    depth, pool, page, H, E = cache_k.shape
    assert E == V7X_LANES and H == V7X_SUBLANES, "one (head, value) tile per position is assumed"
    head_dim = E // 2
    W = H * E
    C = conv_w.shape[2]
    past = page_table.shape[1] * page
    Mp, Ms = B * T, DB * S

    cos_p, sin_p = _rope_tables(np.arange(T), head_dim)
    cos_s, sin_s = _rope_tables(np.tile(past + np.arange(S), DB), head_dim)
    ck = cache_k.reshape(depth, pool, page * H, E)
    cv = cache_v.reshape(depth, pool, page * H, E)

    bm_attn = _pick_block(T, 1024)
    bm_conv = _pick_block(T, 1024)
    bm_out = _pick_block(T, 512)
    bm_ff = _pick_block(T, 2048)
    bm_down = _pick_block(T, 1024)
    blk = _pick_block(T, 512)
    bc = _pick_block(C, 512)
    bn = _pick_block(D, 512)
    bn_ff = _pick_block(w_gate.shape[2], 512)

    xp = x_prompt.reshape(Mp, D)
    xs = x_sample.reshape(Ms, D)
    xpg, pssq = _prep(xp, attn_norm[0], bm=bm_attn)
    xsg, sssq = _prep(xs, attn_norm[0], bm=Ms)
    kv_prompt = (jnp.zeros((depth, Mp * H, E), F32), jnp.zeros((depth, Mp * H, E), F32))
    cp_l, ks_l, vs_l, cs_l = [], [], [], []
    for l in range(depth):
        lam_init = 0.8 - 0.6 * math.exp(-0.3 * l)
        lamv = jnp.stack([lambda_q1[l], lambda_k1[l], lambda_q2[l], lambda_k2[l]])
        g_next = attn_norm[l + 1] if l + 1 < depth else final_norm

        qkv, k_all, v_all = _attn_proj(xpg, pssq, w_in, cos_p, sin_p, kv_prompt, layer=l, slab=l,
                                       n_slabs=depth, bm=bm_attn, n_heads=H)
        kv_prompt = (k_all, v_all)
        yc, tail = _conv_proj(xpg, pssq, w_in, conv_w, None, layer=l, bm=bm_conv, bc=bc, seq=T)
        cp_l.append(tail[:, V7X_SUBLANES - (CONV_K - 1):])

        sqkv, k_new, v_new = _attn_proj(xsg, sssq, w_in, cos_s, sin_s, None, layer=l, slab=0, n_slabs=1,
                                        bm=Ms, n_heads=H)
        k_new = k_new.reshape(DB, S, H, E)
        v_new = v_new.reshape(DB, S, H, E)
        ycs, u = _conv_proj(xsg, sssq, w_in, conv_w, _sample_conv_prev(state_conv[l], S),
                            layer=l, bm=Ms, bc=bc, seq=S)

        ao, aos = _attention(qkv, sqkv[:, :W].astype(F32).reshape(DB, S, H, E), k_new, v_new, ck, cv,
                             page_table, lamv, subln_g[l], layer=l, batch=B, blk=blk, lam_init=lam_init)
        xp, xpg, pssq, xs, xsg, sssq = _out_proj(ao, yc, xp, aos.reshape(Ms, W).astype(BF16), ycs, xs,
                                                 w_out, ffn_norm[l], layer=l, bm=bm_out)
        act_p, act_s, wd = _gate_up(xpg, pssq, xsg, sssq, w_gate, w_up, w_down, layer=l, bm=bm_ff, bn=bn_ff)
        xp, xpg, pssq = _down(act_p, wd, xp, g_next, bm=bm_down, bn=bn)
        xs, xsg, sssq = _down(act_s, wd, xs, g_next, bm=Ms, bn=bn)
        ks_l.append(k_new)
        vs_l.append(v_new)
        cs_l.append(u.reshape(DB, S, C)[:, S - (CONV_K - 1):])

    y_prompt = _final_norm(xp, final_norm, bm=bm_out).reshape(B, T, D)
    y_sample = _final_norm(xs, final_norm, bm=Ms).reshape(DB, S, D)
    k_prompt, v_prompt = (a.reshape(depth, B, T, H, E) for a in kv_prompt)
    return (y_prompt, y_sample, k_prompt, v_prompt, jnp.stack(cp_l),
            jnp.stack(ks_l), jnp.stack(vs_l), jnp.stack(cs_l))
```

```python
import functools
import math

import numpy as np
import jax
import jax.numpy as jnp
from jax import lax
from jax.experimental import pallas as pl
from jax.experimental.pallas import tpu as pltpu

ROPE_THETA = 10000.0
RMS_EPS = 1e-6
SUBLN_EPS = 1e-5
CONV_K = 3
MASK_VALUE = -1e30
SAMPLE_PAGE_GROUP = 16
LOG2_E = math.log2(math.e)

V7X_LANES = 128
V7X_SUBLANES = 8
V7X_VMEM_BYTES = 64 * 1024 * 1024
MXU_WIDTH = 256
VMEM_TEMP_BYTES = 12 * 1024 * 1024

F32 = jnp.float32
BF16 = jnp.bfloat16
NT_DIMS = (((1,), (1,)), ((), ()))


def _compiler_params(semantics, block_bytes):
    limit = min(block_bytes + VMEM_TEMP_BYTES, V7X_VMEM_BYTES - 6 * 1024 * 1024)
    return pltpu.CompilerParams(dimension_semantics=semantics, vmem_limit_bytes=int(limit))


def _weight_buffering(n_row_tiles):
    return pl.Buffered(1) if n_row_tiles > 1 else None


def _nbytes(shape, dtype):
    return int(np.prod(shape)) * jnp.dtype(dtype).itemsize


def _rms_scale(x, eps):
    return x * lax.rsqrt(jnp.mean(x * x, axis=-1, keepdims=True) + eps)


def _row_rsqrt(ssq_ref, width):
    return lax.rsqrt(jnp.sum(ssq_ref[...], axis=1, keepdims=True) * (1.0 / width) + RMS_EPS)


def _lane_fold(x, op):
    return functools.reduce(op, [x[:, j * V7X_LANES:(j + 1) * V7X_LANES] for j in range(x.shape[1] // V7X_LANES)])


def _emit_norm_inputs(x, g_ref, xg_ref, ssq_ref):
    xg_ref[...] = (x * g_ref[...]).astype(BF16)
    ssq_ref[...] = _lane_fold(x * x, jnp.add)


def _lambda_value(lamv_ref, lam_init):
    d1 = jnp.sum(lamv_ref[0:1, :] * lamv_ref[1:2, :], axis=1, keepdims=True)
    d2 = jnp.sum(lamv_ref[2:3, :] * lamv_ref[3:4, :], axis=1, keepdims=True)
    return jnp.exp(d1) - jnp.exp(d2) + lam_init


def _head_finish(acc, l, lam, g, out_scale, rows):
    o = acc[0:rows] / l[0:rows] - lam * (acc[rows:2 * rows] / l[rows:2 * rows])
    return (_rms_scale(o, SUBLN_EPS) * g) * out_scale


def _prep_kernel(x_ref, g_ref, xg_ref, ssq_ref):
    _emit_norm_inputs(x_ref[...], g_ref, xg_ref, ssq_ref)


def _prep(x, g, *, bm):
    M, D = x.shape
    rows = lambda width: pl.BlockSpec((bm, width), lambda m: (m, 0))
    return pl.pallas_call(
        _prep_kernel,
        out_shape=[jax.ShapeDtypeStruct((M, D), BF16), jax.ShapeDtypeStruct((M, V7X_LANES), F32)],
        grid=(M // bm,),
        in_specs=[rows(D), pl.BlockSpec((1, D), lambda m: (0, 0))],
        out_specs=[rows(D), rows(V7X_LANES)],
        compiler_params=_compiler_params(("arbitrary",), 3 * _nbytes((bm, D), F32)),
        name="prep",
    )(x, g.reshape(1, D))


def _rope_tables(positions, head_dim):
    inv = 1.0 / (ROPE_THETA ** (np.arange(0, head_dim, 2, dtype=np.float64) / head_dim))
    ang = np.asarray(positions, np.float64)[:, None] * inv[None, :]
    ang = np.concatenate([ang, ang], axis=-1)
    sign = np.concatenate([-np.ones(head_dim // 2), np.ones(head_dim // 2)])
    reps = V7X_LANES // head_dim
    cos = np.tile(np.cos(ang), (1, reps)).astype(np.float32)
    sin = np.tile(np.sin(ang) * sign[None, :], (1, reps)).astype(np.float32)
    return jnp.asarray(cos), jnp.asarray(sin)


def _rope(zc, cos, sin, low, half):
    ahead = pltpu.roll(zc, V7X_LANES - half, axis=1)
    behind = pltpu.roll(zc, half, axis=1)
    return zc * cos + jnp.where(low, ahead, behind) * sin


def _attn_proj_kernel(xg_ref, ssq_ref, w_ref, cos_ref, sin_ref, *refs, half, q_scale):
    qkv_ref, k_ref, v_ref, w_scr = refs[-4:]
    n = pl.program_id(0)
    bm, W = qkv_ref.shape
    E = k_ref.shape[1]
    H = W // E

    @pl.when(pl.program_id(1) == 0)
    def _():
        w_scr[...] = w_ref[...].astype(BF16)

    r = _row_rsqrt(ssq_ref, xg_ref.shape[1])
    lane = lax.broadcasted_iota(jnp.int32, (bm, E), 1)
    low = (lane % (2 * half)) < half

    def heads():
        per_group = MXU_WIDTH // E
        for c in range(H // per_group):
            cols = slice(c * MXU_WIDTH, (c + 1) * MXU_WIDTH)
            z = jnp.dot(xg_ref[...], w_scr[:, cols], preferred_element_type=F32) * r
            for j in range(per_group):
                yield c * per_group + j, z[:, j * E:(j + 1) * E]

    @pl.when(n == 0)
    def _():
        for h, z in heads():
            q = _rope(z, cos_ref[...], sin_ref[...], low, half)
            qkv_ref[:, h * E:(h + 1) * E] = (q * q_scale).astype(BF16)

    @pl.when(n == 1)
    def _():
        for h, z in heads():
            k = _rope(z, cos_ref[...], sin_ref[...], low, half)
            qkv_ref[:, h * E:(h + 1) * E] = k.astype(BF16)
            k_ref[pl.ds(h, bm, stride=H), :] = k

    @pl.when(n == 2)
    def _():
        for h, z in heads():
            qkv_ref[:, h * E:(h + 1) * E] = z.astype(BF16)
            v_ref[pl.ds(h, bm, stride=H), :] = z


def _attn_proj(xg, ssq, w_in, cos, sin, kv_all, *, layer, slab, n_slabs, bm, n_heads):
    M, D = xg.shape
    W = w_in.shape[2] // 6
    H = n_heads
    E = W // H
    head_dim = E // 2
    n_row_tiles = M // bm
    n_pos_blocks = cos.shape[0] // bm
    k_rows = lambda n, m: (slab, jnp.where(n == 1, m, jnp.where(n < 1, 0, n_row_tiles - 1)), 0)
    v_rows = lambda n, m: (slab, jnp.where(n == 2, m, 0), 0)
    stacked = jax.ShapeDtypeStruct((n_slabs, M * H, E), F32)
    aliases = {} if kv_all is None else {5: 1, 6: 2}
    extra_in = [] if kv_all is None else [pl.BlockSpec(memory_space=pl.ANY)] * 2
    block_bytes = 2 * (_nbytes((bm, D), BF16) + _nbytes((bm, W), BF16) + 2 * _nbytes((bm, W), F32)) \
        + _nbytes((D, W), F32) + _nbytes((D, W), BF16) + 4 * _nbytes((bm, MXU_WIDTH), F32)
    return pl.pallas_call(
        functools.partial(_attn_proj_kernel, half=head_dim // 2, q_scale=head_dim ** -0.5 * LOG2_E),
        out_shape=[jax.ShapeDtypeStruct((M, 3 * W), BF16), stacked, stacked],
        grid=(3, n_row_tiles),
        in_specs=[
            pl.BlockSpec((bm, D), lambda n, m: (m, 0)),
            pl.BlockSpec((bm, ssq.shape[1]), lambda n, m: (m, 0)),
            pl.BlockSpec((None, D, W), lambda n, m: (layer, 0, n), pipeline_mode=_weight_buffering(n_row_tiles)),
            pl.BlockSpec((bm, V7X_LANES), lambda n, m: (m % n_pos_blocks, 0)),
            pl.BlockSpec((bm, V7X_LANES), lambda n, m: (m % n_pos_blocks, 0)),
        ] + extra_in,
        out_specs=[
            pl.BlockSpec((bm, W), lambda n, m: (m, n)),
            pl.BlockSpec((None, bm * H, E), k_rows),
            pl.BlockSpec((None, bm * H, E), v_rows),
        ],
        scratch_shapes=[pltpu.VMEM((D, W), BF16)],
        input_output_aliases=aliases,
        compiler_params=_compiler_params(("arbitrary", "arbitrary"), block_bytes),
        name="attn_proj",
    )(xg, ssq, w_in, cos, sin, *([] if kv_all is None else kv_all))


def _conv_proj_kernel(xg_ref, ssq_ref, wb_ref, wc_ref, wh_ref, cw_ref, *refs, seq, tiles_per_seq):
    carried = tiles_per_seq is not None
    if carried:
        y_ref, tail_ref, wb_scr, wc_scr, wh_scr, carry_scr = refs
    else:
        prev1_ref, prev2_ref, y_ref, tail_ref, wb_scr, wc_scr, wh_scr = refs
    m = pl.program_id(1)
    bm, bc = y_ref.shape

    @pl.when(m == 0)
    def _():
        wb_scr[...] = wb_ref[...].astype(BF16)
        wc_scr[...] = wc_ref[...].astype(BF16)
        wh_scr[...] = wh_ref[...].astype(BF16)

    r = _row_rsqrt(ssq_ref, xg_ref.shape[1])
    if carried:
        @pl.when(m % tiles_per_seq == 0)
        def _():
            carry_scr[...] = jnp.zeros(carry_scr.shape, F32)

    bw = min(bc, MXU_WIDTH)
    t = lax.broadcasted_iota(jnp.int32, (bm, bw), 0)
    for c in range(bc // bw):
        cols = slice(c * bw, (c + 1) * bw)
        gb = jnp.dot(xg_ref[...], wb_scr[:, cols], preferred_element_type=F32) * r
        gc = jnp.dot(xg_ref[...], wc_scr[:, cols], preferred_element_type=F32) * r
        hc = jnp.dot(xg_ref[...], wh_scr[:, cols], preferred_element_type=F32) * r
        u = gc * hc
        if carried:
            last = jnp.broadcast_to(carry_scr[V7X_SUBLANES - 1:V7X_SUBLANES, cols], (bm, bw))
            second_last = jnp.broadcast_to(carry_scr[V7X_SUBLANES - 2:V7X_SUBLANES - 1, cols], (bm, bw))
            prev1 = last
            prev2 = jnp.where(t == 0, second_last, last)
            pos = t
        else:
            prev1 = prev1_ref[:, cols]
            prev2 = prev2_ref[:, cols]
            pos = t % seq
        sh1 = jnp.where(pos >= 1, pltpu.roll(u, 1, axis=0), prev1)
        sh2 = jnp.where(pos >= 2, pltpu.roll(u, 2, axis=0), prev2)
        y = gb * (cw_ref[0:1, cols] * sh2 + cw_ref[1:2, cols] * sh1 + cw_ref[2:3, cols] * u)
        y_ref[:, cols] = y.astype(BF16)
        if carried:
            tail = u[bm - V7X_SUBLANES:bm, :]
            carry_scr[:, cols] = tail
            tail_ref[:, cols] = tail
        else:
            tail_ref[:, cols] = u


def _conv_proj(xg, ssq, w_in, conv_w, prev, *, layer, bm, bc, seq):
    M, D = xg.shape
    C = conv_w.shape[2]
    col0 = (w_in.shape[2] - 3 * C) // bc
    w_spec = lambda j: pl.BlockSpec((None, D, bc), lambda c, m: (layer, 0, col0 + j * (C // bc) + c),
                                    pipeline_mode=_weight_buffering(M // bm))
    tile = pl.BlockSpec((bm, bc), lambda c, m: (m, c))
    in_specs = [
        pl.BlockSpec((bm, D), lambda c, m: (m, 0)),
        pl.BlockSpec((bm, ssq.shape[1]), lambda c, m: (m, 0)),
        w_spec(0), w_spec(1), w_spec(2),
        pl.BlockSpec((None, CONV_K, bc), lambda c, m: (layer, 0, c)),
    ]
    scratch = [pltpu.VMEM((D, bc), BF16)] * 3
    if prev is None:
        tiles_per_seq = seq // bm
        batch = M // seq
        tail_shape = jax.ShapeDtypeStruct((batch, V7X_SUBLANES, C), F32)
        tail_spec = pl.BlockSpec((None, V7X_SUBLANES, bc), lambda c, m: (m // tiles_per_seq, 0, c))
        scratch = scratch + [pltpu.VMEM((V7X_SUBLANES, bc), F32)]
        operands = ()
    else:
        tiles_per_seq = None
        tail_shape = jax.ShapeDtypeStruct((M, C), F32)
        tail_spec = tile
        in_specs = in_specs + [tile, tile]
        operands = prev
    block_bytes = 2 * (_nbytes((bm, D), BF16) + 4 * _nbytes((bm, bc), F32)) \
        + 3 * _nbytes((D, bc), F32) + 3 * _nbytes((D, bc), BF16) + 8 * _nbytes((bm, MXU_WIDTH), F32)
    return pl.pallas_call(
        functools.partial(_conv_proj_kernel, seq=seq, tiles_per_seq=tiles_per_seq),
        out_shape=[jax.ShapeDtypeStruct((M, C), BF16), tail_shape],
        grid=(C // bc, M // bm),
        in_specs=in_specs,
        out_specs=[tile, tail_spec],
        scratch_shapes=scratch,
        compiler_params=_compiler_params(("arbitrary", "arbitrary"), block_bytes),
        name="conv_proj",
    )(xg, ssq, w_in, w_in, w_in, conv_w, *operands)


def _sample_conv_prev(state, seq):
    DB, _, C = state.shape
    zeros = jnp.zeros((DB, seq - 1, C), F32)
    prev1 = jnp.concatenate([state[:, 1:2], zeros], axis=1).reshape(DB * seq, C)
    prev2 = jnp.concatenate([state, zeros[:, 1:]], axis=1).reshape(DB * seq, C)
    return prev1, prev2


def _prompt_query_block(qi, lam, g_col_ref, q_ref, k_ref, vt_scr, o_ref, *, blk, lam_init):
    E = q_ref.shape[1]
    half = E // 2
    qt = q_ref[qi * blk:(qi + 1) * blk, :].astype(F32).T
    feature = lax.broadcasted_iota(jnp.int32, (E, blk), 0)
    qst = jnp.concatenate([jnp.where(feature < half, qt, 0.0), jnp.where(feature >= half, qt, 0.0)],
                          axis=1).astype(BF16)
    key = lax.broadcasted_iota(jnp.int32, (blk, 2 * blk), 0)
    query = lax.broadcasted_iota(jnp.int32, (blk, 2 * blk), 1)
    causal = key <= jnp.where(query >= blk, query - blk, query)

    m = jnp.full((1, 2 * blk), MASK_VALUE, F32)
    l = jnp.zeros((1, 2 * blk), F32)
    acc = jnp.zeros((E, 2 * blk), F32)
    for c in range(qi + 1):
        s = jnp.dot(k_ref[c * blk:(c + 1) * blk, :], qst, preferred_element_type=F32)
        if c == qi:
            s = jnp.where(causal, s, MASK_VALUE)
        m_new = jnp.maximum(m, jnp.max(s, axis=0, keepdims=True))
        alpha = jnp.exp2(m - m_new)
        p = jnp.exp2(s - m_new)
        l = alpha * l + jnp.sum(p, axis=0, keepdims=True)
        acc = alpha * acc + jnp.dot(vt_scr[:, c * blk:(c + 1) * blk], p.astype(BF16), preferred_element_type=F32)
        m = m_new
    ot = acc[:, 0:blk] / l[:, 0:blk] - lam * (acc[:, blk:2 * blk] / l[:, blk:2 * blk])
    ot = ot * lax.rsqrt(jnp.mean(ot * ot, axis=0, keepdims=True) + SUBLN_EPS)
    ot = (ot * g_col_ref[...]) * (1.0 - lam_init)
    o_ref[qi * blk:(qi + 1) * blk, :] = ot.T.astype(o_ref.dtype)


def _sample_start(q_ref, kn_ref, vn_ref, qall_scr, bias_scr, m_scr, l_scr, acc_scr):
    S, H, E = q_ref.shape
    half = E // 2
    rr = lax.broadcasted_iota(jnp.int32, bias_scr.shape, 0)
    cc = lax.broadcasted_iota(jnp.int32, bias_scr.shape, 1)
    bias_scr[...] = jnp.where(cc % H == rr % H, 0.0, MASK_VALUE)
    lane = lax.broadcasted_iota(jnp.int32, (H, E), 1)
    for m in range(2):
        for s in range(S):
            rows = pl.ds((m * S + s) * H, H)
            q = jnp.where(lane // half == m, q_ref[s], 0.0)
            qall_scr[rows, :] = q.astype(BF16)
            sc = [jnp.sum(q * kn_ref[j], axis=1, keepdims=True) for j in range(s + 1)]
            m0 = functools.reduce(jnp.maximum, sc)
            ps = [jnp.exp2(c - m0) for c in sc]
            m_scr[rows, :] = m0
            l_scr[rows, :] = functools.reduce(jnp.add, ps)
            acc_scr[rows, :] = functools.reduce(jnp.add, [p * vn_ref[j] for j, p in enumerate(ps)])


def _sample_pages(k_refs, v_refs, qall_scr, bias_scr, m_scr, l_scr, acc_scr):
    state = (m_scr[...], l_scr[...], acc_scr[...])
    for first in range(0, len(k_refs), SAMPLE_PAGE_GROUP):
        group = slice(first, first + SAMPLE_PAGE_GROUP)
        state = _sample_page_group(k_refs[group], v_refs[group], qall_scr, bias_scr, *state)
    m_scr[...], l_scr[...], acc_scr[...] = state


def _sample_page_group(k_refs, v_refs, qall_scr, bias_scr, m_old, l_old, acc_old):
    qall = qall_scr[...]
    bias = bias_scr[...]
    scores = []
    for k_ref in k_refs:
        kj = k_ref[...].astype(BF16)
        scores.append(lax.dot_general(qall, kj, NT_DIMS, preferred_element_type=F32) + bias)
    m_new = m_old
    for s in scores:
        m_new = jnp.maximum(m_new, jnp.max(s, axis=1, keepdims=True))
    alpha = jnp.exp2(m_old - m_new)
    l_new = alpha * l_old
    acc = alpha * acc_old
    for s, v_ref in zip(scores, v_refs):
        p = jnp.exp2(s - m_new)
        l_new = l_new + jnp.sum(p, axis=1, keepdims=True)
        acc = acc + jnp.dot(p.astype(BF16), v_ref[...].astype(BF16), preferred_element_type=F32)
    return m_new, l_new, acc


def _attention_kernel(pt_ref, lamv_ref, g_ref, g_col_ref, q_ref, k_ref, v_ref, sq_ref, kn_ref, vn_ref, *refs,
                      blk, pages_per_step, steps_per_seq, lam_init):
    P = pages_per_step
    kp_refs, vp_refs = refs[:P], refs[P:2 * P]
    o_ref, so_ref, vt_scr, qall_scr, bias_scr, m_scr, l_scr, acc_scr = refs[2 * P:]
    sample_state = (qall_scr, bias_scr, m_scr, l_scr, acc_scr)
    S, H, E = sq_ref.shape
    n_blocks = q_ref.shape[0] // blk
    n_sub = n_blocks // 2
    step = pl.program_id(0)
    lam = _lambda_value(lamv_ref, lam_init)

    @pl.when(step % steps_per_seq == 0)
    def _():
        _sample_start(sq_ref, kn_ref, vn_ref, *sample_state)

    for j in range(n_sub):
        @pl.when(step % n_sub == j)
        def _(j=j):
            if j == 0:
                vt_scr[...] = v_ref[...].astype(F32).T.astype(BF16)
            _sample_pages(kp_refs, vp_refs, *sample_state)
            for qi in (j, n_blocks - 1 - j):
                _prompt_query_block(qi, lam, g_col_ref, q_ref, k_ref, vt_scr, o_ref, blk=blk, lam_init=lam_init)

    @pl.when(step % steps_per_seq == steps_per_seq - 1)
    def _():
        o = _head_finish(acc_scr[...], l_scr[...], lam, g_ref[...], 1.0 - lam_init, S * H)
        so_ref[...] = o.reshape(S, H, E)


def _attention(qkv, sq, k_new, v_new, cache_k, cache_v, page_table, lamv, g, *, layer, batch, blk, lam_init):
    M = qkv.shape[0]
    DB, S, H, E = sq.shape
    T = M // batch
    page_rows = cache_k.shape[2]
    n_pages = page_table.shape[1]
    n_sub = T // blk // 2
    n_steps = batch * H * n_sub
    assert T % (2 * blk) == 0 and (DB * n_pages) % n_steps == 0, "the page stream must split evenly over the steps"
    P = DB * n_pages // n_steps
    assert n_pages % P == 0
    steps_per_seq = n_pages // P
    R = 2 * S * H
    head = lambda offset: pl.BlockSpec((T, E), lambda i, pt: (i // (H * n_sub), offset + (i // n_sub) % H))
    per_seq = pl.BlockSpec((None, S, H, E), lambda i, pt: (i // steps_per_seq, 0, 0, 0))

    def page_spec(j):
        return pl.BlockSpec((None, None, page_rows, E),
                            lambda i, pt: (layer, pt[i // steps_per_seq, (i % steps_per_seq) * P + j], 0, 0))

    block_bytes = 2 * (2 * P * _nbytes((page_rows, E), F32) + 4 * _nbytes((T, E), BF16)) \
        + _nbytes((E, T), BF16) + 3 * _nbytes((blk, 2 * blk), F32) \
        + (2 + SAMPLE_PAGE_GROUP) * _nbytes((R, page_rows), F32)
    return pl.pallas_call(
        functools.partial(_attention_kernel, blk=blk, pages_per_step=P, steps_per_seq=steps_per_seq,
                          lam_init=lam_init),
        out_shape=[jax.ShapeDtypeStruct((M, H * E), BF16), jax.ShapeDtypeStruct((DB, S, H, E), F32)],
        grid_spec=pltpu.PrefetchScalarGridSpec(
            num_scalar_prefetch=1,
            grid=(n_steps,),
            in_specs=[
                pl.BlockSpec(lamv.shape, lambda i, pt: (0, 0)),
                pl.BlockSpec((1, E), lambda i, pt: (0, 0)),
                pl.BlockSpec((E, 1), lambda i, pt: (0, 0)),
                head(0), head(H), head(2 * H),
                per_seq, per_seq, per_seq,
            ] + [page_spec(j) for j in range(P)] + [page_spec(j) for j in range(P)],
            out_specs=[head(0), per_seq],
            scratch_shapes=[
                pltpu.VMEM((E, T), BF16),
                pltpu.VMEM((R, E), BF16), pltpu.VMEM((R, page_rows), F32),
                pltpu.VMEM((R, 1), F32), pltpu.VMEM((R, 1), F32), pltpu.VMEM((R, E), F32),
            ],
        ),
        compiler_params=_compiler_params(("arbitrary",), block_bytes),
        name="attention",
    )(page_table, lamv, g.reshape(1, E), g.reshape(E, 1), qkv, qkv, qkv, sq, k_new, v_new,
      *([cache_k] * P), *([cache_v] * P))


def _residual_tile(lhs_refs, w_refs, r_ref, g_ref, x_ref, xg_ref, ssq_ref):
    bn = x_ref.shape[1]
    bw = min(bn, MXU_WIDTH)
    ssq = None
    for c in range(bn // bw):
        cols = slice(c * bw, (c + 1) * bw)
        x = r_ref[:, cols]
        for lhs_ref, w_ref in zip(lhs_refs, w_refs):
            x = x + jnp.dot(lhs_ref[...], w_ref[:, cols], preferred_element_type=F32)
        x_ref[:, cols] = x
        xg_ref[:, cols] = (x * g_ref[:, cols]).astype(BF16)
        part = _lane_fold(x * x, jnp.add)
        ssq = part if ssq is None else ssq + part
    ssq_ref[...] = ssq


def _out_proj_kernel(a_ref, y_ref, r_ref, as_ref, ys_ref, rs_ref, w_ref, g_ref,
                     x_ref, xg_ref, ssq_ref, xs_ref, xsg_ref, sssq_ref, w_scr):
    ka = a_ref.shape[1]
    weights = (w_scr.at[0:ka, :], w_scr.at[ka:, :])

    @pl.when(pl.program_id(0) == 0)
    def _():
        w_scr[...] = w_ref[...].astype(BF16)
        _residual_tile((as_ref, ys_ref), weights, rs_ref, g_ref, xs_ref, xsg_ref, sssq_ref)

    _residual_tile((a_ref, y_ref), weights, r_ref, g_ref, x_ref, xg_ref, ssq_ref)


def _out_proj(a, y, res, a_s, y_s, res_s, w, g, *, layer, bm):
    M, ka = a.shape
    Ms = a_s.shape[0]
    K = ka + y.shape[1]
    N = w.shape[2]
    rows = lambda width: pl.BlockSpec((bm, width), lambda m: (m, 0))
    whole = lambda width: pl.BlockSpec((Ms, width), lambda m: (0, 0))
    block_bytes = 2 * (_nbytes((bm + Ms, K), BF16) + 2 * _nbytes((bm + Ms, N), F32) + _nbytes((bm + Ms, N), BF16)) \
        + _nbytes((K, N), F32) + _nbytes((K, N), BF16) + 4 * _nbytes((bm, MXU_WIDTH), F32)
    out = lambda rows_: [jax.ShapeDtypeStruct((rows_, N), F32), jax.ShapeDtypeStruct((rows_, N), BF16),
                         jax.ShapeDtypeStruct((rows_, V7X_LANES), F32)]
    return pl.pallas_call(
        _out_proj_kernel,
        out_shape=out(M) + out(Ms),
        grid=(M // bm,),
        in_specs=[
            rows(ka), rows(K - ka), rows(N),
            whole(ka), whole(K - ka), whole(N),
            pl.BlockSpec((None, K, N), lambda m: (layer, 0, 0), pipeline_mode=pl.Buffered(1)),
            pl.BlockSpec((1, N), lambda m: (0, 0)),
        ],
        out_specs=[rows(N), rows(N), rows(V7X_LANES), whole(N), whole(N), whole(V7X_LANES)],
        scratch_shapes=[pltpu.VMEM((K, N), BF16)],
        compiler_params=_compiler_params(("arbitrary",), block_bytes),
        name="out_proj",
    )(a, y, res, a_s, y_s, res_s, w, g.reshape(1, N))


def _swiglu_tile(xg_ref, ssq_ref, wg_scr, wu_scr, o_ref):
    r = _row_rsqrt(ssq_ref, xg_ref.shape[1])
    bn = o_ref.shape[1]
    bw = min(bn, MXU_WIDTH)
    for c in range(bn // bw):
        cols = slice(c * bw, (c + 1) * bw)
        gate = jnp.dot(xg_ref[...], wg_scr[:, cols], preferred_element_type=F32) * r
        up = jnp.dot(xg_ref[...], wu_scr[:, cols], preferred_element_type=F32) * r
        o_ref[:, cols] = ((gate / (1.0 + jnp.exp(-gate))) * up).astype(BF16)


def _gate_up_kernel(xg_ref, ssq_ref, xsg_ref, sssq_ref, wg_ref, wu_ref, wd_ref,
                    o_ref, os_ref, wdb_ref, wg_scr, wu_scr):
    @pl.when(pl.program_id(1) == 0)
    def _():
        wg_scr[...] = wg_ref[...].astype(BF16)
        wu_scr[...] = wu_ref[...].astype(BF16)
        _swiglu_tile(xsg_ref, sssq_ref, wg_scr, wu_scr, os_ref)

    wdb_ref[...] = wd_ref[...].astype(BF16)
    _swiglu_tile(xg_ref, ssq_ref, wg_scr, wu_scr, o_ref)


def _gate_up(xg, ssq, xsg, sssq, wg, wu, wd, *, layer, bm, bn):
    M, D = xg.shape
    Ms = xsg.shape[0]
    F, N = wd.shape[1:]
    n_m = M // bm
    slab = F // (F // bn * n_m)
    w_spec = pl.BlockSpec((None, D, bn), lambda n, m: (layer, 0, n))
    block_bytes = 2 * (_nbytes((bm, D), BF16) + 2 * _nbytes((D, bn), F32) + _nbytes((bm, bn), BF16)
                       + _nbytes((slab, N), F32) + _nbytes((slab, N), BF16) + _nbytes((Ms, D), BF16)) \
        + 2 * _nbytes((D, bn), BF16) + 6 * _nbytes((bm, MXU_WIDTH), F32)
    return pl.pallas_call(
        _gate_up_kernel,
        out_shape=[jax.ShapeDtypeStruct((M, F), BF16), jax.ShapeDtypeStruct((Ms, F), BF16),
                   jax.ShapeDtypeStruct((F, N), BF16)],
        grid=(F // bn, n_m),
        in_specs=[
            pl.BlockSpec((bm, D), lambda n, m: (m, 0)),
            pl.BlockSpec((bm, ssq.shape[1]), lambda n, m: (m, 0)),
            pl.BlockSpec((Ms, D), lambda n, m: (0, 0)),
            pl.BlockSpec((Ms, sssq.shape[1]), lambda n, m: (0, 0)),
            w_spec, w_spec,
            pl.BlockSpec((None, slab, N), lambda n, m: (layer, n * n_m + m, 0)),
        ],
        out_specs=[
            pl.BlockSpec((bm, bn), lambda n, m: (m, n)),
            pl.BlockSpec((Ms, bn), lambda n, m: (0, n)),
            pl.BlockSpec((slab, N), lambda n, m: (n * n_m + m, 0)),
        ],
        scratch_shapes=[pltpu.VMEM((D, bn), BF16)] * 2,
        compiler_params=_compiler_params(("arbitrary", "arbitrary"), block_bytes),
        name="gate_up",
    )(xg, ssq, xsg, sssq, wg, wu, wd)


def _down_kernel(a_ref, w_ref, r_ref, g_ref, x_ref, xg_ref, ssq_ref):
    _residual_tile((a_ref,), (w_ref,), r_ref, g_ref, x_ref, xg_ref, ssq_ref)


def _down(a, w, res, g, *, bm, bn):
    M, F = a.shape
    N = w.shape[1]
    tile = lambda width: pl.BlockSpec((bm, width), lambda m, n: (m, n))
    block_bytes = 2 * (_nbytes((bm, F), BF16) + _nbytes((F, bn), BF16) + 2 * _nbytes((bm, bn), F32)
                       + _nbytes((bm, bn), BF16)) + 4 * _nbytes((bm, MXU_WIDTH), F32)
    return pl.pallas_call(
        _down_kernel,
        out_shape=[jax.ShapeDtypeStruct((M, N), F32), jax.ShapeDtypeStruct((M, N), BF16),
                   jax.ShapeDtypeStruct((M, N // bn * V7X_LANES), F32)],
        grid=(M // bm, N // bn),
        in_specs=[
            pl.BlockSpec((bm, F), lambda m, n: (m, 0)),
            pl.BlockSpec((F, bn), lambda m, n: (0, n)),
            tile(bn),
            pl.BlockSpec((1, bn), lambda m, n: (0, n)),
        ],
        out_specs=[tile(bn), tile(bn), tile(V7X_LANES)],
        compiler_params=_compiler_params(("arbitrary", "arbitrary"), block_bytes),
        name="down_proj",
    )(a, w, res, g.reshape(1, N))


def _down_final_kernel(a_ref, w_ref, r_ref, g_ref, y_ref):
    N = y_ref.shape[1]
    ssq = None
    for c in range(N // MXU_WIDTH):
        cols = slice(c * MXU_WIDTH, (c + 1) * MXU_WIDTH)
        x = r_ref[:, cols] + jnp.dot(a_ref[...], w_ref[:, cols], preferred_element_type=F32)
        y_ref[:, cols] = x
        part = _lane_fold(x * x, jnp.add)
        ssq = part if ssq is None else ssq + part
    r = lax.rsqrt(jnp.sum(ssq, axis=1, keepdims=True) * (1.0 / N) + RMS_EPS)
    y_ref[...] = (y_ref[...] * r) * g_ref[...]


def _down_final(a, w, res, g, *, bm):
    M, F = a.shape
    N = w.shape[1]
    rows = lambda width: pl.BlockSpec((bm, width), lambda m: (m, 0))
    block_bytes = 2 * (_nbytes((bm, F), BF16) + 2 * _nbytes((bm, N), F32)) + _nbytes((F, N), BF16) \
        + 4 * _nbytes((bm, MXU_WIDTH), F32)
    return pl.pallas_call(
        _down_final_kernel,
        out_shape=jax.ShapeDtypeStruct((M, N), F32),
        grid=(M // bm,),
        in_specs=[
            rows(F),
            pl.BlockSpec((F, N), lambda m: (0, 0), pipeline_mode=pl.Buffered(1)),
            rows(N),
            pl.BlockSpec((1, N), lambda m: (0, 0)),
        ],
        out_specs=rows(N),
        compiler_params=_compiler_params(("arbitrary",), block_bytes),
        name="down_final",
    )(a, w, res, g.reshape(1, N))


def _pick_block(size, preferred):
    b = min(size, preferred)
    while size % b:
        b //= 2
    return b


def kernel(x_prompt, x_sample, cache_k, cache_v, state_conv, page_table, attn_norm, w_in, conv_w,
           lambda_q1, lambda_k1, lambda_q2, lambda_k2, subln_g, w_out, ffn_norm, w_gate, w_up,
           w_down, final_norm):
    B, T, D = x_prompt.shape
    DB, S, _ = x_sample.shape
    depth, pool, page, H, E = cache_k.shape
    assert E == V7X_LANES and H == V7X_SUBLANES, "one (head, value) tile per position is assumed"
    head_dim = E // 2
    W = H * E
    C = conv_w.shape[2]
    past = page_table.shape[1] * page
    Mp, Ms = B * T, DB * S

    cos_p, sin_p = _rope_tables(np.arange(T), head_dim)
    cos_s, sin_s = _rope_tables(np.tile(past + np.arange(S), DB), head_dim)
    ck = cache_k.reshape(depth, pool, page * H, E)
    cv = cache_v.reshape(depth, pool, page * H, E)

    bm_attn = _pick_block(T, 1024)
    bm_conv = _pick_block(T, 1024)
    bm_out = _pick_block(T, 512)
    bm_ff = _pick_block(T, 2048)
    bm_down = _pick_block(T, 1024)
    blk = _pick_block(T, 512)
    bc = _pick_block(C, 512)
    bn = _pick_block(D, 512)
    bn_ff = _pick_block(w_gate.shape[2], 512)

    xp = x_prompt.reshape(Mp, D)
    xs = x_sample.reshape(Ms, D)
    xpg, pssq = _prep(xp, attn_norm[0], bm=bm_attn)
    xsg, sssq = _prep(xs, attn_norm[0], bm=Ms)
    kv_prompt = (jnp.zeros((depth, Mp * H, E), F32), jnp.zeros((depth, Mp * H, E), F32))
    cp_l, ks_l, vs_l, cs_l = [], [], [], []
    for l in range(depth):
        lam_init = 0.8 - 0.6 * math.exp(-0.3 * l)
        lamv = jnp.stack([lambda_q1[l], lambda_k1[l], lambda_q2[l], lambda_k2[l]])
        last = l + 1 == depth

        qkv, k_all, v_all = _attn_proj(xpg, pssq, w_in, cos_p, sin_p, kv_prompt, layer=l, slab=l,
                                       n_slabs=depth, bm=bm_attn, n_heads=H)
        kv_prompt = (k_all, v_all)
        yc, tail = _conv_proj(xpg, pssq, w_in, conv_w, None, layer=l, bm=bm_conv, bc=bc, seq=T)
        cp_l.append(tail[:, V7X_SUBLANES - (CONV_K - 1):])

        sqkv, k_new, v_new = _attn_proj(xsg, sssq, w_in, cos_s, sin_s, None, layer=l, slab=0, n_slabs=1,
                                        bm=Ms, n_heads=H)
        k_new = k_new.reshape(DB, S, H, E)
        v_new = v_new.reshape(DB, S, H, E)
        ycs, u = _conv_proj(xsg, sssq, w_in, conv_w, _sample_conv_prev(state_conv[l], S),
                            layer=l, bm=Ms, bc=bc, seq=S)

        ao, aos = _attention(qkv, sqkv[:, :W].astype(F32).reshape(DB, S, H, E), k_new, v_new, ck, cv,
                             page_table, lamv, subln_g[l], layer=l, batch=B, blk=blk, lam_init=lam_init)
        xp, xpg, pssq, xs, xsg, sssq = _out_proj(ao, yc, xp, aos.reshape(Ms, W).astype(BF16), ycs, xs,
                                                 w_out, ffn_norm[l], layer=l, bm=bm_out)
        act_p, act_s, wd = _gate_up(xpg, pssq, xsg, sssq, w_gate, w_up, w_down, layer=l, bm=bm_ff, bn=bn_ff)
        if last:
            y_prompt = _down_final(act_p, wd, xp, final_norm, bm=bm_out).reshape(B, T, D)
            y_sample = _down_final(act_s, wd, xs, final_norm, bm=Ms).reshape(DB, S, D)
        else:
            xp, xpg, pssq = _down(act_p, wd, xp, attn_norm[l + 1], bm=bm_down, bn=bn)
            xs, xsg, sssq = _down(act_s, wd, xs, attn_norm[l + 1], bm=Ms, bn=bn)
        ks_l.append(k_new)
        vs_l.append(v_new)
        cs_l.append(u.reshape(DB, S, C)[:, S - (CONV_K - 1):])

    k_prompt, v_prompt = (a.reshape(depth, B, T, H, E) for a in kv_prompt)
    return (y_prompt, y_sample, k_prompt, v_prompt, jnp.stack(cp_l),
            jnp.stack(ks_l), jnp.stack(vs_l), jnp.stack(cs_l))
```
